```python
import math
import jax, jax.numpy as jnp
from jax import lax
import numpy as np

D_MODEL = 1024
BATCH = 8
SEQ = 4096
DEPTH = 2

HEAD_DIM = 64
A_Q_HEADS = 8
A_KV_HEADS = 2
A_GROUP = A_Q_HEADS // A_KV_HEADS
WINDOW = 128
BLOCK = 128
NUM_BUCKETS = 32
MAX_DISTANCE = 128
B_HEADS = 8
A_WIDTH = A_Q_HEADS * HEAD_DIM
A_KV_WIDTH = A_KV_HEADS * HEAD_DIM
B_WIDTH = B_HEADS * HEAD_DIM
DECAY_LORA = 64
AAA_LORA = 64
GATE_LORA = 128
B_IN_WIDTH = 3 * B_WIDTH + DECAY_LORA + AAA_LORA + GATE_LORA
EVEN_IN_WIDTH = A_WIDTH + 2 * A_KV_WIDTH + B_IN_WIDTH
MIX_WIDTH = A_WIDTH + B_WIDTH
CONV_WIDTH = 3
CONV_DIM = D_MODEL
D_FF_DENSE = 2816
N_EXPERTS = 8
TOP_K = 2
D_FF_EXPERT = 3584
N_EVEN = (DEPTH + 1) // 2
N_ODD = DEPTH // 2
ALPHA = (2.0 * DEPTH) ** 0.25
BETA = (8.0 * DEPTH) ** -0.25
LN_EPS = 1e-5
GN_EPS = 64e-5
NEG_INF = -1e30
ATTN_SCALE = HEAD_DIM ** -0.5

kernel_name = "hybrid_swa_rwkv7_shortconv_moe_deepnorm"


def layer_norm(x, g, b):
    xf = x.astype(jnp.float32)
    mu = xf.mean(-1, keepdims=True)
    var = jnp.square(xf - mu).mean(-1, keepdims=True)
    y = (xf - mu) * lax.rsqrt(var + LN_EPS) * g.astype(jnp.float32) + b.astype(jnp.float32)
    return y.astype(x.dtype)


def token_shift(z):
    return jnp.pad(z, ((0, 0), (1, 0), (0, 0)))[:, :-1]


def t5_causal_bucket(dist):
    n = jnp.maximum(dist, 0)
    max_exact = NUM_BUCKETS // 2
    log_ratio = jnp.log(jnp.maximum(n, 1).astype(jnp.float32) / max_exact) / math.log(MAX_DISTANCE / max_exact)
    large = max_exact + (log_ratio * (NUM_BUCKETS - max_exact)).astype(jnp.int32)
    large = jnp.minimum(large, NUM_BUCKETS - 1)
    return jnp.where(n < max_exact, n, large)


def sliding_window_sink_attention(q, k, v, sinks, rel_bias_table):
    bsz, seq, _ = q.shape
    nb = seq // BLOCK
    qb = q.reshape(bsz, nb, BLOCK, A_KV_HEADS, A_GROUP, HEAD_DIM)
    kb = k.reshape(bsz, nb, BLOCK, A_KV_HEADS, HEAD_DIM)
    vb = v.reshape(bsz, nb, BLOCK, A_KV_HEADS, HEAD_DIM)
    pad = ((0, 0), (1, 0), (0, 0), (0, 0), (0, 0))
    kcat = jnp.concatenate([jnp.pad(kb, pad)[:, :-1], kb], axis=2)
    vcat = jnp.concatenate([jnp.pad(vb, pad)[:, :-1], vb], axis=2)
    qi = jnp.arange(BLOCK)[:, None]
    ki = jnp.arange(2 * BLOCK)[None, :]
    dist = qi + BLOCK - ki
    s_abs = jnp.arange(nb)[:, None, None] * BLOCK - BLOCK + ki[None]
    valid = (dist >= 0) & (dist < WINDOW) & (s_abs >= 0)
    bias = rel_bias_table[t5_causal_bucket(dist)].astype(jnp.float32)
    bias = jnp.transpose(bias, (2, 0, 1)).reshape(A_KV_HEADS, A_GROUP, BLOCK, 2 * BLOCK)
    scores = jnp.einsum('bnqhgd,bnshd->bnhgqs', qb, kcat).astype(jnp.float32) * ATTN_SCALE + bias
    scores = jnp.where(valid[None, :, None, None], scores, NEG_INF)
    sink = sinks.astype(jnp.float32).reshape(A_KV_HEADS, A_GROUP)[None, None, :, :, None]
    m = jnp.maximum(scores.max(-1), sink)
    p = jnp.exp(scores - m[..., None])
    denom = p.sum(-1) + jnp.exp(sink - m)
    p = p / denom[..., None]
    out = jnp.einsum('bnhgqs,bnshd->bnqhgd', p.astype(v.dtype), vcat)
    return out.reshape(bsz, seq, A_WIDTH)


def rwkv7_scan(r, w, k, v, a, b):
    bsz, _, nh, hd = r.shape
    xs = tuple(jnp.moveaxis(t, 1, 0) for t in (r, w, k, v, a, b))

    def step(state, inp):
        r_t, w_t, k_t, v_t, a_t, b_t = inp
        sa = jnp.einsum('bhij,bhj->bhi', state, a_t)
        state = state * w_t[:, :, None, :] + sa[..., None] * b_t[:, :, None, :] + v_t[..., None] * k_t[:, :, None, :]
        y = jnp.einsum('bhij,bhj->bhi', state, r_t)
        return state, y

    state0 = jnp.zeros((bsz, nh, hd, hd), jnp.float32)
    _, ys = lax.scan(step, state0, xs)
    return jnp.moveaxis(ys, 0, 1)


def rwkv7_time_mix(zb, mu, w0, w_up, a0, a_up, g_up, k_k, k_a, r_k, gn_g, gn_b):
    bsz, seq, _ = zb.shape
    f32 = jnp.float32
    zb = zb + (token_shift(zb) - zb) * mu
    splits = [B_WIDTH, 2 * B_WIDTH, 3 * B_WIDTH, 3 * B_WIDTH + DECAY_LORA, 3 * B_WIDTH + DECAY_LORA + AAA_LORA]
    zr, zk, zv, zwd, zad, zgd = jnp.split(zb, splits, axis=-1)
    w = -jax.nn.softplus(-(w0 + jnp.tanh(zwd) @ w_up).astype(f32)) - 0.5
    decay = jnp.exp(-jnp.exp(w))
    a = jax.nn.sigmoid((a0 + zad @ a_up).astype(f32))
    g = jax.nn.sigmoid(zgd) @ g_up

    def heads(t):
        return t.reshape(bsz, seq, B_HEADS, HEAD_DIM)

    zkf = zk.astype(f32)
    kk = heads(zkf * k_k.astype(f32))
    kk = kk / jnp.maximum(jnp.sqrt(jnp.sum(kk * kk, axis=-1, keepdims=True)), 1e-12)
    k = heads(zkf * (1.0 + (a - 1.0) * k_a.astype(f32)))
    r = heads(zr.astype(f32))
    v = heads(zv.astype(f32))
    a_h = heads(a)
    y = rwkv7_scan(r, heads(decay), k, v, -kk, kk * a_h)
    mean = y.mean(-1, keepdims=True)
    var = jnp.square(y - mean).mean(-1, keepdims=True)
    yn = ((y - mean) * lax.rsqrt(var + GN_EPS)).reshape(bsz, seq, B_WIDTH) * gn_g.astype(f32) + gn_b.astype(f32)
    bonus = (jnp.sum(r * k * r_k.astype(f32), axis=-1, keepdims=True) * v).reshape(bsz, seq, B_WIDTH)
    out = (yn + bonus) * g.astype(f32)
    return out.astype(zb.dtype)


def attention_rwkv_mixer(x, w_in, sinks, rel_bias_table, mu, w0, w_up, a0, a_up, g_up, k_k, k_a, r_k, gn_g, gn_b, w_out):
    z = x @ w_in
    q, k, v, zb = jnp.split(z, [A_WIDTH, A_WIDTH + A_KV_WIDTH, A_WIDTH + 2 * A_KV_WIDTH], axis=-1)
    ya = sliding_window_sink_attention(q, k, v, sinks, rel_bias_table)
    yb = rwkv7_time_mix(zb, mu, w0, w_up, a0, a_up, g_up, k_k, k_a, r_k, gn_g, gn_b)
    return jnp.concatenate([ya.astype(x.dtype), yb.astype(x.dtype)], axis=-1) @ w_out


def short_conv_mixer(x, w_in, conv_w, w_out):
    seq = x.shape[1]
    gb, gc, u = jnp.split(x @ w_in, 3, axis=-1)
    u = gc * u
    up = jnp.pad(u, ((0, 0), (CONV_WIDTH - 1, 0), (0, 0)))
    conv = up[:, 0:seq] * conv_w[0]
    for j in range(1, CONV_WIDTH):
        conv = conv + up[:, j:j + seq] * conv_w[j]
    return (gb * conv) @ w_out


def swiglu(x, w_gate, w_up, w_down):
    return (jax.nn.silu(x @ w_gate) * (x @ w_up)) @ w_down


def moe_swiglu(x, router_w, w_gate, w_up, w_down):
    bsz, seq, d = x.shape
    xt = x.reshape(-1, d)
    logits = (xt @ router_w).astype(jnp.float32)
    top_vals, top_idx = lax.top_k(logits, TOP_K)
    top_p = jax.nn.softmax(top_vals, axis=-1)
    combine = jnp.einsum('tk,tke->te', top_p, jax.nn.one_hot(top_idx, N_EXPERTS, dtype=jnp.float32))
    out = jnp.zeros_like(xt)
    for e in range(N_EXPERTS):
        y_e = swiglu(xt, w_gate[e], w_up[e], w_down[e])
        out = out + combine[:, e:e + 1].astype(x.dtype) * y_e
    return out.reshape(bsz, seq, d)


def setup_inputs(seed: int = 0) -> dict:
    key = jax.random.key(seed)
    ks = iter(jax.random.split(key, 48))

    def nrm(shape, scale):
        return jax.random.normal(next(ks), shape, jnp.float32) * scale

    def gain(shape):
        return 1.0 + nrm(shape, 0.05)

    def unif(shape, lo, hi):
        return jax.random.uniform(next(ks), shape, jnp.float32, lo, hi)

    D = D_MODEL
    return {
        "x": nrm((BATCH, SEQ, D), 1.0),
        "rel_bias_table": nrm((NUM_BUCKETS, A_Q_HEADS), 0.5),
        "even_w_in": nrm((N_EVEN, D, EVEN_IN_WIDTH), D ** -0.5),
        "even_sinks": nrm((N_EVEN, A_Q_HEADS), 0.5),
        "rwkv_mu": unif((N_EVEN, B_IN_WIDTH), 0.0, 1.0),
        "rwkv_w0": unif((N_EVEN, B_WIDTH), -6.0, 1.0),
        "rwkv_w_up": nrm((N_EVEN, DECAY_LORA, B_WIDTH), 0.5 * DECAY_LORA ** -0.5),
        "rwkv_a0": nrm((N_EVEN, B_WIDTH), 0.5),
        "rwkv_a_up": nrm((N_EVEN, AAA_LORA, B_WIDTH), 0.5 * AAA_LORA ** -0.5),
        "rwkv_g_up": nrm((N_EVEN, GATE_LORA, B_WIDTH), GATE_LORA ** -0.5),
        "rwkv_k_k": 0.85 + nrm((N_EVEN, B_WIDTH), 0.05),
        "rwkv_k_a": 1.0 + nrm((N_EVEN, B_WIDTH), 0.05),
        "rwkv_r_k": nrm((N_EVEN, B_HEADS, HEAD_DIM), 0.1),
        "rwkv_gn_g": gain((N_EVEN, B_WIDTH)),
        "rwkv_gn_b": nrm((N_EVEN, B_WIDTH), 0.02),
        "even_w_out": nrm((N_EVEN, MIX_WIDTH, D), BETA * MIX_WIDTH ** -0.5),
        "even_ln_mix_g": gain((N_EVEN, D)),
        "even_ln_mix_b": nrm((N_EVEN, D), 0.02),
        "dense_w_gate": nrm((N_EVEN, D, D_FF_DENSE), D ** -0.5),
        "dense_w_up": nrm((N_EVEN, D, D_FF_DENSE), D ** -0.5),
        "dense_w_down": nrm((N_EVEN, D_FF_DENSE, D), BETA * D_FF_DENSE ** -0.5),
        "even_ln_ffn_g": gain((N_EVEN, D)),
        "even_ln_ffn_b": nrm((N_EVEN, D), 0.02),
        "odd_w_in": nrm((N_ODD, D, 3 * CONV_DIM), D ** -0.5),
        "odd_conv_w": nrm((N_ODD, CONV_WIDTH, CONV_DIM), CONV_WIDTH ** -0.5),
        "odd_w_out": nrm((N_ODD, CONV_DIM, D), BETA * CONV_DIM ** -0.5),
        "odd_ln_mix_g": gain((N_ODD, D)),
        "odd_ln_mix_b": nrm((N_ODD, D), 0.02),
        "router_w": nrm((N_ODD, D, N_EXPERTS), D ** -0.5),
        "moe_w_gate": nrm((N_ODD, N_EXPERTS, D, D_FF_EXPERT), D ** -0.5),
        "moe_w_up": nrm((N_ODD, N_EXPERTS, D, D_FF_EXPERT), D ** -0.5),
        "moe_w_down": nrm((N_ODD, N_EXPERTS, D_FF_EXPERT, D), BETA * D_FF_EXPERT ** -0.5),
        "odd_ln_ffn_g": gain((N_ODD, D)),
        "odd_ln_ffn_b": nrm((N_ODD, D), 0.02),
    }


def reference(x, rel_bias_table, even_w_in, even_sinks, rwkv_mu, rwkv_w0, rwkv_w_up, rwkv_a0, rwkv_a_up, rwkv_g_up, rwkv_k_k, rwkv_k_a, rwkv_r_k, rwkv_gn_g, rwkv_gn_b, even_w_out, even_ln_mix_g, even_ln_mix_b, dense_w_gate, dense_w_up, dense_w_down, even_ln_ffn_g, even_ln_ffn_b, odd_w_in, odd_conv_w, odd_w_out, odd_ln_mix_g, odd_ln_mix_b, router_w, moe_w_gate, moe_w_up, moe_w_down, odd_ln_ffn_g, odd_ln_ffn_b):
    h = x
    for layer in range(DEPTH):
        i = layer // 2
        if layer % 2 == 0:
            mix = attention_rwkv_mixer(h, even_w_in[i], even_sinks[i], rel_bias_table, rwkv_mu[i], rwkv_w0[i], rwkv_w_up[i], rwkv_a0[i], rwkv_a_up[i], rwkv_g_up[i], rwkv_k_k[i], rwkv_k_a[i], rwkv_r_k[i], rwkv_gn_g[i], rwkv_gn_b[i], even_w_out[i])
            h = layer_norm(ALPHA * h + mix, even_ln_mix_g[i], even_ln_mix_b[i])
            ffn = swiglu(h, dense_w_gate[i], dense_w_up[i], dense_w_down[i])
            h = layer_norm(ALPHA * h + ffn, even_ln_ffn_g[i], even_ln_ffn_b[i])
        else:
            mix = short_conv_mixer(h, odd_w_in[i], odd_conv_w[i], odd_w_out[i])
            h = layer_norm(ALPHA * h + mix, odd_ln_mix_g[i], odd_ln_mix_b[i])
            ffn = moe_swiglu(h, router_w[i], moe_w_gate[i], moe_w_up[i], moe_w_down[i])
            h = layer_norm(ALPHA * h + ffn, odd_ln_ffn_g[i], odd_ln_ffn_b[i])
    return h
```

```python
import functools
import math

import numpy as np
import jax
import jax.numpy as jnp
from jax import lax
from jax.experimental import pallas as pl
from jax.experimental.pallas import tpu as pltpu

F32 = jnp.float32
BF16 = jnp.bfloat16

HEAD_DIM = 64
A_Q_HEADS = 8
A_KV_HEADS = 2
A_GROUP = A_Q_HEADS // A_KV_HEADS
WINDOW = 128
BLOCK = 128
NUM_BUCKETS = 32
MAX_DISTANCE = 128
B_HEADS = 8
A_WIDTH = A_Q_HEADS * HEAD_DIM
A_KV_WIDTH = A_KV_HEADS * HEAD_DIM
QKV_WIDTH = A_WIDTH + 2 * A_KV_WIDTH
B_WIDTH = B_HEADS * HEAD_DIM
DECAY_LORA = 64
AAA_LORA = 64
GATE_LORA = 128
B_IN_WIDTH = 3 * B_WIDTH + DECAY_LORA + AAA_LORA + GATE_LORA
CONV_WIDTH = 3
N_EXPERTS = 8
TOP_K = 2
DEPTH = 2
ALPHA = (2.0 * DEPTH) ** 0.25
LN_EPS = 1e-5
GN_EPS = 64e-5
NEG_INF = -1e30
ATTN_SCALE = HEAD_DIM ** -0.5

V7X_LANES = 128
V7X_SUBLANES = 8
V7X_VMEM_LIMIT = 56 * 1024 * 1024

TOKEN_TILE = 512
SCAN_STEPS = 16
MOE_ROW_TILE = 512
GATHER_TILE = 256
ROUTER_LANES = 128


def _cparams(*sem):
    return pltpu.CompilerParams(dimension_semantics=sem, vmem_limit_bytes=V7X_VMEM_LIMIT)


def _layer_norm(y, g, b):
    mu = jnp.mean(y, axis=-1, keepdims=True)
    d = y - mu
    var = jnp.mean(d * d, axis=-1, keepdims=True)
    return d * lax.rsqrt(var + LN_EPS) * g + b


def _dot(a, b):
    return jnp.dot(a, b, preferred_element_type=F32)


def _head_sum(x, ones_blockdiag):
    hi = x.astype(BF16)
    lo = (x - hi.astype(F32)).astype(BF16)
    return _dot(hi, ones_blockdiag) + _dot(lo, ones_blockdiag)


def _in_proj_kernel(x_ref, w_ref, qkv_ref, zb_ref):
    z = _dot(x_ref[...].astype(BF16), w_ref[...])
    qkv_ref[...] = z[:, :QKV_WIDTH].astype(BF16)
    zb_ref[...] = z[:, QKV_WIDTH:]


def _in_proj(h, w):
    t, d = h.shape
    n = w.shape[1]
    tm = TOKEN_TILE
    return pl.pallas_call(
        _in_proj_kernel,
        grid=(t // tm,),
        in_specs=[pl.BlockSpec((tm, d), lambda i: (i, 0)),
                  pl.BlockSpec((d, n), lambda i: (0, 0))],
        out_specs=[pl.BlockSpec((tm, QKV_WIDTH), lambda i: (i, 0)),
                   pl.BlockSpec((tm, n - QKV_WIDTH), lambda i: (i, 0))],
        out_shape=[jax.ShapeDtypeStruct((t, QKV_WIDTH), BF16),
                   jax.ShapeDtypeStruct((t, n - QKV_WIDTH), F32)],
        compiler_params=_cparams("parallel"),
        name="in_proj",
    )(h, w)


def _bucket_table():
    qi = np.arange(BLOCK)[:, None]
    ki = np.arange(2 * BLOCK)[None, :]
    n = np.maximum(qi + BLOCK - ki, 0)
    max_exact = NUM_BUCKETS // 2
    log_ratio = (np.log(np.maximum(n, 1).astype(np.float32) / max_exact)
                 / math.log(MAX_DISTANCE / max_exact))
    large = max_exact + (log_ratio * (NUM_BUCKETS - max_exact)).astype(np.int32)
    large = np.minimum(large, NUM_BUCKETS - 1)
    return np.where(n < max_exact, n, large).astype(np.int32)


def _bias_kernel(tab_ref, bucket_ref, out_ref):
    bucket = bucket_ref[...]
    for h in range(A_Q_HEADS):
        acc = jnp.zeros(bucket.shape, F32)
        for b in range(NUM_BUCKETS):
            acc = jnp.where(bucket == b, tab_ref[b, h], acc)
        out_ref[h] = acc


def _rel_bias(table):
    return pl.pallas_call(
        _bias_kernel,
        in_specs=[pl.BlockSpec(memory_space=pltpu.SMEM),
                  pl.BlockSpec(memory_space=pltpu.VMEM)],
        out_specs=pl.BlockSpec(memory_space=pltpu.VMEM),
        out_shape=jax.ShapeDtypeStruct((A_Q_HEADS, BLOCK, 2 * BLOCK), F32),
        name="rel_bias",
    )(table, jnp.asarray(_bucket_table()))


def _attn_kernel(sink_ref, q_ref, kp_ref, kc_ref, vp_ref, vc_ref, bias_ref, o_ref):
    n = pl.program_id(1)
    qi = lax.broadcasted_iota(jnp.int32, (BLOCK, 2 * BLOCK), 0)
    ki = lax.broadcasted_iota(jnp.int32, (BLOCK, 2 * BLOCK), 1)
    dist = qi + BLOCK - ki
    valid = (dist >= 0) & (dist < WINDOW) & ((ki >= BLOCK) | (n > 0))
    q = q_ref[...]
    kcat = jnp.concatenate([kp_ref[...], kc_ref[...]], axis=0)
    vcat = jnp.concatenate([vp_ref[...], vc_ref[...]], axis=0)
    outs = []
    for hk in range(A_KV_HEADS):
        k_h = kcat[:, hk * HEAD_DIM:(hk + 1) * HEAD_DIM]
        v_h = vcat[:, hk * HEAD_DIM:(hk + 1) * HEAD_DIM]
        for g in range(A_GROUP):
            hq = hk * A_GROUP + g
            q_h = q[:, hq * HEAD_DIM:(hq + 1) * HEAD_DIM]
            s = lax.dot_general(q_h, k_h, (((1,), (1,)), ((), ())),
                                preferred_element_type=F32)
            s = s * ATTN_SCALE + bias_ref[hq]
            s = jnp.where(valid, s, NEG_INF)
            sink = sink_ref[hq]
            m = jnp.maximum(jnp.max(s, axis=-1, keepdims=True), sink)
            p = jnp.exp(s - m)
            denom = jnp.sum(p, axis=-1, keepdims=True) + jnp.exp(sink - m)
            o = _dot(p.astype(BF16), v_h)
            outs.append(o / denom)
    o_ref[...] = jnp.concatenate(outs, axis=-1).astype(o_ref.dtype)


def _attention(qkv, sinks, bias, bsz, seq):
    t = qkv.shape[0]
    nb = seq // BLOCK
    kcol = A_WIDTH // A_KV_WIDTH
    vcol = kcol + 1

    def cur(b, n):
        return b * nb + n

    def prev(b, n):
        return b * nb + jnp.maximum(n - 1, 0)

    return pl.pallas_call(
        _attn_kernel,
        grid=(bsz, nb),
        in_specs=[pl.BlockSpec(memory_space=pltpu.SMEM),
                  pl.BlockSpec((BLOCK, A_WIDTH), lambda b, n: (cur(b, n), 0)),
                  pl.BlockSpec((BLOCK, A_KV_WIDTH), lambda b, n: (prev(b, n), kcol)),
                  pl.BlockSpec((BLOCK, A_KV_WIDTH), lambda b, n: (cur(b, n), kcol)),
                  pl.BlockSpec((BLOCK, A_KV_WIDTH), lambda b, n: (prev(b, n), vcol)),
                  pl.BlockSpec((BLOCK, A_KV_WIDTH), lambda b, n: (cur(b, n), vcol)),
                  pl.BlockSpec((A_Q_HEADS, BLOCK, 2 * BLOCK), lambda b, n: (0, 0, 0))],
        out_specs=pl.BlockSpec((BLOCK, A_WIDTH), lambda b, n: (cur(b, n), 0)),
        out_shape=jax.ShapeDtypeStruct((t, A_WIDTH), BF16),
        compiler_params=_cparams("parallel", "parallel"),
        name="swa_attention",
    )(sinks, qkv, qkv, qkv, qkv, qkv, bias)


def _rwkv_prep_kernel(seq, zb_ref, halo_ref, mu_ref, w0_ref, wup_ref, a0_ref, aup_ref,
                      gup_ref, kk_ref, ka_ref, rk_ref, ones_ref,
                      p_ref, v_ref, bonus_ref, g_ref):
    tm = zb_ref.shape[0]
    i = pl.program_id(0)
    z = zb_ref[...]
    seq_start = (i * tm) % seq == 0
    prev_last = jnp.where(seq_start, 0.0, halo_ref[V7X_SUBLANES - 1:V7X_SUBLANES, :])
    row = lax.broadcasted_iota(jnp.int32, z.shape, 0)
    zsh = jnp.where(row == 0, prev_last, pltpu.roll(z, 1, axis=0))
    zs = z + (zsh - z) * mu_ref[...]
    c = B_WIDTH
    zr, zk, zv = zs[:, :c], zs[:, c:2 * c], zs[:, 2 * c:3 * c]
    zwd = zs[:, 3 * c:3 * c + DECAY_LORA]
    zad = zs[:, 3 * c + DECAY_LORA:3 * c + DECAY_LORA + AAA_LORA]
    zgd = zs[:, 3 * c + DECAY_LORA + AAA_LORA:]
    ones = ones_ref[...]

    w = -jax.nn.softplus(-(w0_ref[...] + _dot(jnp.tanh(zwd).astype(BF16), wup_ref[...]))) - 0.5
    decay = jnp.exp(-jnp.exp(w))
    a = jax.nn.sigmoid(a0_ref[...] + _dot(zad.astype(BF16), aup_ref[...]))
    g = _dot(jax.nn.sigmoid(zgd).astype(BF16), gup_ref[...])

    kk = zk * kk_ref[...]
    kk = kk / jnp.maximum(jnp.sqrt(_head_sum(kk * kk, ones)), 1e-12)
    k = zk * (1.0 + (a - 1.0) * ka_ref[...])
    p_ref[0] = zr
    p_ref[1] = decay
    p_ref[2] = k
    p_ref[3] = -kk
    p_ref[4] = kk * a
    v_ref[...] = zv
    bonus_ref[...] = _head_sum(zr * k * rk_ref[...], ones) * zv
    g_ref[...] = g


def _rwkv_prep(zb, seq, mu, w0, w_up, a0, a_up, g_up, k_k, k_a, r_k, ones):
    t, n = zb.shape
    tm = TOKEN_TILE
    hb = tm // V7X_SUBLANES
    c = B_WIDTH

    def row(v):
        return v.reshape(1, -1)

    def full(a):
        return pl.BlockSpec(a.shape, lambda i: (0,) * a.ndim)

    small = [row(mu), row(w0), w_up.astype(BF16), row(a0), a_up.astype(BF16),
             g_up.astype(BF16), row(k_k), row(k_a), row(r_k), ones]
    tok = pl.BlockSpec((tm, c), lambda i: (i, 0))
    return pl.pallas_call(
        functools.partial(_rwkv_prep_kernel, seq),
        grid=(t // tm,),
        in_specs=[pl.BlockSpec((tm, n), lambda i: (i, 0)),
                  pl.BlockSpec((V7X_SUBLANES, n), lambda i: (jnp.maximum(i * hb - 1, 0), 0))]
                 + [full(a) for a in small],
        out_specs=[pl.BlockSpec((5, tm, c), lambda i: (0, i, 0)), tok, tok, tok],
        out_shape=[jax.ShapeDtypeStruct((5, t, c), F32)]
                  + [jax.ShapeDtypeStruct((t, c), F32)] * 3,
        compiler_params=_cparams("parallel"),
        name="rwkv_prep",
    )(zb, zb, *small)


def _rwkv_scan_kernel(p_ref, v_ref, y_ref, s_ref):
    @pl.when(pl.program_id(0) == 0)
    def _():
        s_ref[...] = jnp.zeros_like(s_ref)

    def vec(t, which, j):
        return p_ref[t, which, pl.ds(j, 1), :]

    def step(t, carry):
        v_t = v_ref[t]
        sa = [jnp.zeros_like(v_t), jnp.zeros_like(v_t)]
        for j in range(HEAD_DIM):
            sa[j % 2] = sa[j % 2] + s_ref[j] * vec(t, 3, j)
        sa = sa[0] + sa[1]
        y = [jnp.zeros_like(v_t), jnp.zeros_like(v_t)]
        for j in range(HEAD_DIM):
            s = s_ref[j] * vec(t, 1, j) + sa * vec(t, 4, j) + v_t * vec(t, 2, j)
            s_ref[j] = s
            y[j % 2] = y[j % 2] + s * vec(t, 0, j)
        y_ref[t] = y[0] + y[1]
        return carry

    lax.fori_loop(0, p_ref.shape[0], step, 0)


def _rwkv_scan(p, v):
    seq = p.shape[0]
    tt = SCAN_STEPS
    half = HEAD_DIM // 2
    return pl.pallas_call(
        _rwkv_scan_kernel,
        grid=(seq // tt,),
        in_specs=[pl.BlockSpec((tt, 5, HEAD_DIM, V7X_LANES), lambda i: (i, 0, 0, 0)),
                  pl.BlockSpec((tt, half, V7X_LANES), lambda i: (i, 0, 0))],
        out_specs=pl.BlockSpec((tt, half, V7X_LANES), lambda i: (i, 0, 0)),
        out_shape=jax.ShapeDtypeStruct((seq, half, V7X_LANES), F32),
        scratch_shapes=[pltpu.VMEM((HEAD_DIM, half, V7X_LANES), F32)],
        compiler_params=_cparams("arbitrary"),
        name="rwkv_scan",
    )(p, v)


def _mix_out_kernel(ya_ref, y_ref, bonus_ref, g_ref, h_ref, gng_ref, gnb_ref, ones_ref,
                    wa_ref, wb_ref, lng_ref, lnb_ref, o_ref):
    ones = ones_ref[...]
    y = y_ref[...]
    inv = 1.0 / HEAD_DIM
    d = y - _head_sum(y, ones) * inv
    var = _head_sum(d * d, ones) * inv
    yn = d * lax.rsqrt(var + GN_EPS) * gng_ref[...] + gnb_ref[...]
    yb = (yn + bonus_ref[...]) * g_ref[...]
    mix = _dot(ya_ref[...], wa_ref[...]) + _dot(yb.astype(BF16), wb_ref[...])
    o_ref[...] = _layer_norm(ALPHA * h_ref[...] + mix, lng_ref[...], lnb_ref[...])


def _mix_out(ya, y, bonus, g, h, gn_g, gn_b, ones, w_out, ln_g, ln_b):
    t, d = h.shape
    tm = TOKEN_TILE
    c = B_WIDTH
    wa = w_out[:A_WIDTH].astype(BF16)
    wb = w_out[A_WIDTH:].astype(BF16)
    small = [gn_g.reshape(1, -1), gn_b.reshape(1, -1), ones, wa, wb,
             ln_g.reshape(1, -1), ln_b.reshape(1, -1)]
    tok = pl.BlockSpec((tm, c), lambda i: (i, 0))
    return pl.pallas_call(
        _mix_out_kernel,
        grid=(t // tm,),
        in_specs=[tok, tok, tok, tok, pl.BlockSpec((tm, d), lambda i: (i, 0))]
                 + [pl.BlockSpec(a.shape, lambda i: (0, 0)) for a in small],
        out_specs=pl.BlockSpec((tm, d), lambda i: (i, 0)),
        out_shape=jax.ShapeDtypeStruct((t, d), F32),
        compiler_params=_cparams("parallel"),
        name="mix_out",
    )(ya, y, bonus, g, h, *small)


def _ffn_kernel(x_ref, wg_ref, wu_ref, wd_ref, lng_ref, lnb_ref, o_ref, xb_ref, acc_ref):
    f = pl.program_id(1)

    @pl.when(f == 0)
    def _():
        xb_ref[...] = x_ref[...].astype(BF16)
        acc_ref[...] = jnp.zeros_like(acc_ref)

    xb = xb_ref[...]
    hid = jax.nn.silu(_dot(xb, wg_ref[...])) * _dot(xb, wu_ref[...])
    acc_ref[...] += _dot(hid.astype(BF16), wd_ref[...])

    @pl.when(f == pl.num_programs(1) - 1)
    def _():
        o_ref[...] = _layer_norm(ALPHA * x_ref[...] + acc_ref[...], lng_ref[...], lnb_ref[...])


def _ffn_chunk(d_ff):
    for tf in (512, 256, 128):
        if d_ff % tf == 0:
            return tf
    raise ValueError(f"d_ff={d_ff} is not a multiple of {V7X_LANES}")


def _dense_ffn(x, w_gate, w_up, w_down, ln_g, ln_b):
    t, d = x.shape
    d_ff = w_gate.shape[1]
    tm = TOKEN_TILE
    tf = _ffn_chunk(d_ff)
    return pl.pallas_call(
        _ffn_kernel,
        grid=(t // tm, d_ff // tf),
        in_specs=[pl.BlockSpec((tm, d), lambda i, f: (i, 0)),
                  pl.BlockSpec((d, tf), lambda i, f: (0, f)),
                  pl.BlockSpec((d, tf), lambda i, f: (0, f)),
                  pl.BlockSpec((tf, d), lambda i, f: (f, 0)),
                  pl.BlockSpec((1, d), lambda i, f: (0, 0)),
                  pl.BlockSpec((1, d), lambda i, f: (0, 0))],
        out_specs=pl.BlockSpec((tm, d), lambda i, f: (i, 0)),
        out_shape=jax.ShapeDtypeStruct((t, d), F32),
        scratch_shapes=[pltpu.VMEM((tm, d), BF16), pltpu.VMEM((tm, d), F32)],
        compiler_params=_cparams("parallel", "arbitrary"),
        name="dense_ffn",
    )(x, w_gate.astype(BF16), w_up.astype(BF16), w_down.astype(BF16),
      ln_g.reshape(1, -1), ln_b.reshape(1, -1))


def _conv_mixer_kernel(seq, x_ref, halo_ref, win_ref, cw_ref, wout_ref, lng_ref, lnb_ref, o_ref):
    tm, d = x_ref.shape
    i = pl.program_id(0)
    x = x_ref[...]
    seq_start = (i * tm) % seq == 0
    xa = jnp.concatenate([halo_ref[...], x], axis=0).astype(BF16)
    z = _dot(xa, win_ref[...])
    gb = z[V7X_SUBLANES:, :d]
    u = z[:, d:2 * d] * z[:, 2 * d:]
    row = lax.broadcasted_iota(jnp.int32, u.shape, 0)
    u = jnp.where((row < V7X_SUBLANES) & seq_start, 0.0, u)
    cw = cw_ref[...]
    conv = pltpu.roll(u, 2, axis=0)[V7X_SUBLANES:] * cw[0:1]
    conv = conv + pltpu.roll(u, 1, axis=0)[V7X_SUBLANES:] * cw[1:2]
    conv = conv + u[V7X_SUBLANES:] * cw[2:3]
    y = _dot((gb * conv).astype(BF16), wout_ref[...])
    o_ref[...] = _layer_norm(ALPHA * x + y, lng_ref[...], lnb_ref[...])


def _conv_mixer(x, seq, w_in, conv_w, w_out, ln_g, ln_b):
    t, d = x.shape
    tm = TOKEN_TILE
    hb = tm // V7X_SUBLANES
    small = [w_in.astype(BF16), conv_w, w_out.astype(BF16), ln_g.reshape(1, -1), ln_b.reshape(1, -1)]
    return pl.pallas_call(
        functools.partial(_conv_mixer_kernel, seq),
        grid=(t // tm,),
        in_specs=[pl.BlockSpec((tm, d), lambda i: (i, 0)),
                  pl.BlockSpec((V7X_SUBLANES, d), lambda i: (jnp.maximum(i * hb - 1, 0), 0))]
                 + [pl.BlockSpec(a.shape, lambda i: (0, 0)) for a in small],
        out_specs=pl.BlockSpec((tm, d), lambda i: (i, 0)),
        out_shape=jax.ShapeDtypeStruct((t, d), F32),
        compiler_params=_cparams("parallel"),
        name="conv_mixer",
    )(x, x, *small)


def _router_kernel(x_ref, w_ref, idx_ref, prob_ref):
    logits = jnp.dot(x_ref[...], w_ref[...], preferred_element_type=F32,
                     precision=lax.Precision.HIGHEST)
    lane = lax.broadcasted_iota(jnp.int32, logits.shape, 1)
    logits = jnp.where(lane < N_EXPERTS, logits, -jnp.inf)
    m1 = jnp.max(logits, axis=-1, keepdims=True)
    i1 = jnp.min(jnp.where(logits == m1, lane, ROUTER_LANES), axis=-1, keepdims=True)
    rest = jnp.where(lane == i1, -jnp.inf, logits)
    m2 = jnp.max(rest, axis=-1, keepdims=True)
    i2 = jnp.min(jnp.where(rest == m2, lane, ROUTER_LANES), axis=-1, keepdims=True)
    e2 = jnp.exp(m2 - m1)
    den = 1.0 + e2
    idx_ref[...] = jnp.where(lane == 0, i1, jnp.where(lane == 1, i2, 0))
    prob_ref[...] = jnp.where(lane == 0, 1.0 / den, jnp.where(lane == 1, e2 / den, 0.0))


def _router(x, router_w):
    t, d = x.shape
    tm = TOKEN_TILE
    w = jnp.zeros((d, ROUTER_LANES), F32).at[:, :N_EXPERTS].set(router_w)
    out = pl.BlockSpec((tm, ROUTER_LANES), lambda i: (i, 0))
    idx, prob = pl.pallas_call(
        _router_kernel,
        grid=(t // tm,),
        in_specs=[pl.BlockSpec((tm, d), lambda i: (i, 0)),
                  pl.BlockSpec((d, ROUTER_LANES), lambda i: (0, 0))],
        out_specs=[out, out],
        out_shape=[jax.ShapeDtypeStruct((t, ROUTER_LANES), jnp.int32),
                   jax.ShapeDtypeStruct((t, ROUTER_LANES), F32)],
        compiler_params=_cparams("parallel"),
        name="router",
    )(x, w)
    return idx[:, :TOP_K], prob[:, :TOP_K]


def _gather_kernel(idx_ref, src_ref, o_ref, sem):
    tg = o_ref.shape[0]

    def row_copy(r):
        return pltpu.make_async_copy(src_ref.at[pl.ds(idx_ref[0, 0, r], 1)],
                                     o_ref.at[pl.ds(r, 1)], sem)

    def start(r, c):
        row_copy(r).start()
        return c

    def wait(r, c):
        row_copy(r).wait()
        return c

    lax.fori_loop(0, tg, start, 0)
    lax.fori_loop(0, tg, wait, 0)


def _gather_rows(src, idx):
    r = idx.shape[0]
    d = src.shape[1]
    tg = GATHER_TILE
    return pl.pallas_call(
        _gather_kernel,
        grid=(r // tg,),
        in_specs=[pl.BlockSpec((1, 1, tg), lambda i: (i, 0, 0), memory_space=pltpu.SMEM),
                  pl.BlockSpec(memory_space=pl.ANY)],
        out_specs=pl.BlockSpec((tg, d), lambda i: (i, 0)),
        out_shape=jax.ShapeDtypeStruct((r, d), src.dtype),
        scratch_shapes=[pltpu.SemaphoreType.DMA(())],
        compiler_params=_cparams("arbitrary"),
        name="gather_rows",
    )(idx.reshape(r // tg, 1, tg), src)


def _moe_kernel(te_ref, x_ref, p_ref, wg_ref, wu_ref, wd_ref, o_ref, xb_ref, acc_ref):
    f = pl.program_id(1)

    @pl.when(f == 0)
    def _():
        xb_ref[...] = x_ref[...].astype(BF16)
        acc_ref[...] = jnp.zeros_like(acc_ref)

    x = xb_ref[...]
    hid = jax.nn.silu(_dot(x, wg_ref[0])) * _dot(x, wu_ref[0])
    acc_ref[...] += _dot(hid.astype(BF16), wd_ref[0])

    @pl.when(f == pl.num_programs(1) - 1)
    def _():
        o_ref[...] = p_ref[...] * acc_ref[...]


def _moe_experts(xs, ps, tile_expert, w_gate, w_up, w_down):
    r, d = xs.shape
    d_ff = w_gate.shape[2]
    tg = MOE_ROW_TILE
    tf = _ffn_chunk(d_ff)
    return pl.pallas_call(
        _moe_kernel,
        grid_spec=pltpu.PrefetchScalarGridSpec(
            num_scalar_prefetch=1,
            grid=(r // tg, d_ff // tf),
            in_specs=[pl.BlockSpec((tg, d), lambda i, f, te: (i, 0)),
                      pl.BlockSpec((tg, 1), lambda i, f, te: (i, 0)),
                      pl.BlockSpec((1, d, tf), lambda i, f, te: (te[i], 0, f)),
                      pl.BlockSpec((1, d, tf), lambda i, f, te: (te[i], 0, f)),
                      pl.BlockSpec((1, tf, d), lambda i, f, te: (te[i], f, 0))],
            out_specs=pl.BlockSpec((tg, d), lambda i, f, te: (i, 0)),
            scratch_shapes=[pltpu.VMEM((tg, d), BF16), pltpu.VMEM((tg, d), F32)]),
        out_shape=jax.ShapeDtypeStruct((r, d), F32),
        compiler_params=_cparams("parallel", "arbitrary"),
        name="moe_experts",
    )(tile_expert, xs, ps, w_gate, w_up, w_down)


def _combine_kernel(pos_ref, h_ref, y_ref, lng_ref, lnb_ref, o_ref, buf_ref, sem):
    tm = h_ref.shape[0]

    def row_copy(r, k):
        return pltpu.make_async_copy(y_ref.at[pl.ds(pos_ref[0, k, r], 1)],
                                     buf_ref.at[k, pl.ds(r, 1)], sem)

    def start(r, c):
        row_copy(r, 0).start()
        row_copy(r, 1).start()
        return c

    def wait(r, c):
        row_copy(r, 0).wait()
        row_copy(r, 1).wait()
        return c

    lax.fori_loop(0, tm, start, 0)
    lax.fori_loop(0, tm, wait, 0)
    ffn = buf_ref[0] + buf_ref[1]
    o_ref[...] = _layer_norm(ALPHA * h_ref[...] + ffn, lng_ref[...], lnb_ref[...])


def _moe_combine(h, y, pos, ln_g, ln_b):
    t, d = h.shape
    tm = GATHER_TILE
    pos_tiles = jnp.transpose(pos.reshape(TOP_K, t // tm, tm), (1, 0, 2))
    return pl.pallas_call(
        _combine_kernel,
        grid=(t // tm,),
        in_specs=[pl.BlockSpec((1, TOP_K, tm), lambda i: (i, 0, 0), memory_space=pltpu.SMEM),
                  pl.BlockSpec((tm, d), lambda i: (i, 0)),
                  pl.BlockSpec(memory_space=pl.ANY),
                  pl.BlockSpec((1, d), lambda i: (0, 0)),
                  pl.BlockSpec((1, d), lambda i: (0, 0))],
        out_specs=pl.BlockSpec((tm, d), lambda i: (i, 0)),
        out_shape=jax.ShapeDtypeStruct((t, d), F32),
        scratch_shapes=[pltpu.VMEM((TOP_K, tm, d), F32), pltpu.SemaphoreType.DMA(())],
        compiler_params=_cparams("arbitrary"),
        name="moe_combine",
    )(pos_tiles, h, y, ln_g.reshape(1, -1), ln_b.reshape(1, -1))


def _route_plan(top_idx, top_p):
    t = top_idx.shape[0]
    tg = MOE_ROW_TILE
    e_flat = top_idx.T.reshape(-1)
    onehot = (e_flat[:, None] == jnp.arange(N_EXPERTS)[None, :]).astype(jnp.int32)
    csum = jnp.cumsum(onehot, axis=0)
    counts = csum[-1]
    rank = jnp.take_along_axis(csum, e_flat[:, None], axis=1)[:, 0] - 1
    padded = (counts + tg - 1) // tg * tg
    ends = jnp.cumsum(padded)
    starts = ends - padded
    pos = starts[e_flat] + rank
    rows = TOP_K * t + N_EXPERTS * tg
    token = jnp.tile(jnp.arange(t, dtype=jnp.int32), TOP_K)
    row_token = jnp.zeros((rows,), jnp.int32).at[pos].set(token)
    row_p = jnp.zeros((rows,), F32).at[pos].set(top_p.T.reshape(-1))
    tile_start = jnp.arange(rows // tg, dtype=jnp.int32) * tg
    tile_expert = jnp.minimum(jnp.searchsorted(ends, tile_start, side="right"),
                              N_EXPERTS - 1).astype(jnp.int32)
    return pos.reshape(TOP_K, t).astype(jnp.int32), row_token, row_p.reshape(-1, 1), tile_expert


def _moe(h, router_w, w_gate, w_up, w_down, ln_g, ln_b):
    top_idx, top_p = _router(h, router_w)
    pos, row_token, row_p, tile_expert = _route_plan(top_idx, top_p)
    xs = _gather_rows(h, row_token)
    y = _moe_experts(xs, row_p, tile_expert,
                     w_gate.astype(BF16), w_up.astype(BF16), w_down.astype(BF16))
    return _moe_combine(h, y, pos, ln_g, ln_b)


def _to_scan_layout(p, v, bsz, seq):
    half = HEAD_DIM // 2
    p = p.reshape(5, bsz, seq, B_HEADS, HEAD_DIM)
    p = jnp.transpose(p, (2, 0, 4, 1, 3)).reshape(seq, 5, HEAD_DIM, bsz * B_HEADS)
    p = jnp.concatenate([p, p], axis=-1)
    v = v.reshape(bsz, seq, B_HEADS, 2, half)
    v = jnp.transpose(v, (1, 4, 3, 0, 2)).reshape(seq, half, 2 * bsz * B_HEADS)
    return p, v


def _from_scan_layout(y, bsz, seq):
    half = HEAD_DIM // 2
    y = y.reshape(seq, half, 2, bsz, B_HEADS)
    return jnp.transpose(y, (3, 0, 4, 2, 1)).reshape(bsz * seq, B_WIDTH)


def kernel(x, rel_bias_table, even_w_in, even_sinks, rwkv_mu, rwkv_w0, rwkv_w_up, rwkv_a0, rwkv_a_up, rwkv_g_up, rwkv_k_k, rwkv_k_a, rwkv_r_k, rwkv_gn_g, rwkv_gn_b, even_w_out, even_ln_mix_g, even_ln_mix_b, dense_w_gate, dense_w_up, dense_w_down, even_ln_ffn_g, even_ln_ffn_b, odd_w_in, odd_conv_w, odd_w_out, odd_ln_mix_g, odd_ln_mix_b, router_w, moe_w_gate, moe_w_up, moe_w_down, odd_ln_ffn_g, odd_ln_ffn_b):
    bsz, seq, d = x.shape
    assert seq % TOKEN_TILE == 0 and seq % BLOCK == 0 and seq % SCAN_STEPS == 0
    assert bsz * B_HEADS * 2 == V7X_LANES, "scan layout puts (half, batch, head) on the lanes"
    h = x.reshape(bsz * seq, d)
    ones = jnp.asarray(np.kron(np.eye(B_HEADS), np.ones((HEAD_DIM, HEAD_DIM))), BF16)
    bias = _rel_bias(rel_bias_table)
    for layer in range(DEPTH):
        i = layer // 2
        if layer % 2 == 0:
            qkv, zb = _in_proj(h, even_w_in[i].astype(BF16))
            ya = _attention(qkv, even_sinks[i], bias, bsz, seq)
            p, v, bonus, g = _rwkv_prep(zb, seq, rwkv_mu[i], rwkv_w0[i], rwkv_w_up[i], rwkv_a0[i],
                                        rwkv_a_up[i], rwkv_g_up[i], rwkv_k_k[i], rwkv_k_a[i],
                                        rwkv_r_k[i], ones)
            y = _from_scan_layout(_rwkv_scan(*_to_scan_layout(p, v, bsz, seq)), bsz, seq)
            h = _mix_out(ya, y, bonus, g, h, rwkv_gn_g[i], rwkv_gn_b[i], ones, even_w_out[i],
                         even_ln_mix_g[i], even_ln_mix_b[i])
            h = _dense_ffn(h, dense_w_gate[i], dense_w_up[i], dense_w_down[i],
                           even_ln_ffn_g[i], even_ln_ffn_b[i])
        else:
            h = _conv_mixer(h, seq, odd_w_in[i], odd_conv_w[i], odd_w_out[i],
                            odd_ln_mix_g[i], odd_ln_mix_b[i])
            h = _moe(h, router_w[i], moe_w_gate[i], moe_w_up[i], moe_w_down[i],
                     odd_ln_ffn_g[i], odd_ln_ffn_b[i])
    return h.reshape(bsz, seq, d)
```

```python
import functools
import math

import numpy as np
import jax
import jax.numpy as jnp
from jax import lax
from jax.experimental import pallas as pl
from jax.experimental.pallas import tpu as pltpu

F32 = jnp.float32
BF16 = jnp.bfloat16

HEAD_DIM = 64
A_Q_HEADS = 8
A_KV_HEADS = 2
A_GROUP = A_Q_HEADS // A_KV_HEADS
WINDOW = 128
BLOCK = 128
NUM_BUCKETS = 32
MAX_DISTANCE = 128
B_HEADS = 8
A_WIDTH = A_Q_HEADS * HEAD_DIM
A_KV_WIDTH = A_KV_HEADS * HEAD_DIM
QKV_WIDTH = A_WIDTH + 2 * A_KV_WIDTH
B_WIDTH = B_HEADS * HEAD_DIM
DECAY_LORA = 64
AAA_LORA = 64
GATE_LORA = 128
B_IN_WIDTH = 3 * B_WIDTH + DECAY_LORA + AAA_LORA + GATE_LORA
CONV_WIDTH = 3
N_EXPERTS = 8
TOP_K = 2
DEPTH = 2
ALPHA = (2.0 * DEPTH) ** 0.25
LN_EPS = 1e-5
GN_EPS = 64e-5
NEG_INF = -1e30
ATTN_SCALE = HEAD_DIM ** -0.5

V7X_LANES = 128
V7X_SUBLANES = 8
V7X_VMEM_LIMIT = 56 * 1024 * 1024

TOKEN_TILE = 512
SCAN_STEPS = 16
MOE_ROW_TILE = 512
GATHER_TILE = 256
ROUTER_LANES = 128


def _cparams(*sem):
    return pltpu.CompilerParams(dimension_semantics=sem, vmem_limit_bytes=V7X_VMEM_LIMIT)


def _layer_norm(y, g, b):
    mu = jnp.mean(y, axis=-1, keepdims=True)
    d = y - mu
    var = jnp.mean(d * d, axis=-1, keepdims=True)
    return d * lax.rsqrt(var + LN_EPS) * g + b


def _dot(a, b):
    return jnp.dot(a, b, preferred_element_type=F32)


def _head_sum(x, ones_blockdiag):
    hi = x.astype(BF16)
    lo = (x - hi.astype(F32)).astype(BF16)
    return _dot(hi, ones_blockdiag) + _dot(lo, ones_blockdiag)


def _in_proj_kernel(x_ref, w_ref, qkv_ref, zb_ref):
    z = _dot(x_ref[...].astype(BF16), w_ref[...])
    qkv_ref[...] = z[:, :QKV_WIDTH].astype(BF16)
    zb_ref[...] = z[:, QKV_WIDTH:]


def _in_proj(h, w):
    t, d = h.shape
    n = w.shape[1]
    tm = TOKEN_TILE
    return pl.pallas_call(
        _in_proj_kernel,
        grid=(t // tm,),
        in_specs=[pl.BlockSpec((tm, d), lambda i: (i, 0)),
                  pl.BlockSpec((d, n), lambda i: (0, 0))],
        out_specs=[pl.BlockSpec((tm, QKV_WIDTH), lambda i: (i, 0)),
                   pl.BlockSpec((tm, n - QKV_WIDTH), lambda i: (i, 0))],
        out_shape=[jax.ShapeDtypeStruct((t, QKV_WIDTH), BF16),
                   jax.ShapeDtypeStruct((t, n - QKV_WIDTH), F32)],
        compiler_params=_cparams("parallel"),
        name="in_proj",
    )(h, w)


def _bucket_table():
    qi = np.arange(BLOCK)[:, None]
    ki = np.arange(2 * BLOCK)[None, :]
    n = np.maximum(qi + BLOCK - ki, 0)
    max_exact = NUM_BUCKETS // 2
    log_ratio = (np.log(np.maximum(n, 1).astype(np.float32) / max_exact)
                 / math.log(MAX_DISTANCE / max_exact))
    large = max_exact + (log_ratio * (NUM_BUCKETS - max_exact)).astype(np.int32)
    large = np.minimum(large, NUM_BUCKETS - 1)
    return np.where(n < max_exact, n, large).astype(np.int32)


def _bias_kernel(tab_ref, bucket_ref, out_ref):
    bucket = bucket_ref[...]
    for h in range(A_Q_HEADS):
        acc = jnp.zeros(bucket.shape, F32)
        for b in range(NUM_BUCKETS):
            acc = jnp.where(bucket == b, tab_ref[b, h], acc)
        out_ref[h] = acc


def _rel_bias(table):
    return pl.pallas_call(
        _bias_kernel,
        in_specs=[pl.BlockSpec(memory_space=pltpu.SMEM),
                  pl.BlockSpec(memory_space=pltpu.VMEM)],
        out_specs=pl.BlockSpec(memory_space=pltpu.VMEM),
        out_shape=jax.ShapeDtypeStruct((A_Q_HEADS, BLOCK, 2 * BLOCK), F32),
        name="rel_bias",
    )(table, jnp.asarray(_bucket_table()))


def _attn_kernel(sink_ref, q_ref, kp_ref, kc_ref, vp_ref, vc_ref, bias_ref, o_ref):
    n = pl.program_id(1)
    qi = lax.broadcasted_iota(jnp.int32, (BLOCK, 2 * BLOCK), 0)
    ki = lax.broadcasted_iota(jnp.int32, (BLOCK, 2 * BLOCK), 1)
    dist = qi + BLOCK - ki
    valid = (dist >= 0) & (dist < WINDOW) & ((ki >= BLOCK) | (n > 0))
    q = q_ref[...]
    kcat = jnp.concatenate([kp_ref[...], kc_ref[...]], axis=0)
    vcat = jnp.concatenate([vp_ref[...], vc_ref[...]], axis=0)
    outs = []
    for hk in range(A_KV_HEADS):
        k_h = kcat[:, hk * HEAD_DIM:(hk + 1) * HEAD_DIM]
        v_h = vcat[:, hk * HEAD_DIM:(hk + 1) * HEAD_DIM]
        for g in range(A_GROUP):
            hq = hk * A_GROUP + g
            q_h = q[:, hq * HEAD_DIM:(hq + 1) * HEAD_DIM]
            s = lax.dot_general(q_h, k_h, (((1,), (1,)), ((), ())),
                                preferred_element_type=F32)
            s = s * ATTN_SCALE + bias_ref[hq]
            s = jnp.where(valid, s, NEG_INF)
            sink = sink_ref[hq]
            m = jnp.maximum(jnp.max(s, axis=-1, keepdims=True), sink)
            p = jnp.exp(s - m)
            denom = jnp.sum(p, axis=-1, keepdims=True) + jnp.exp(sink - m)
            o = _dot(p.astype(BF16), v_h)
            outs.append(o / denom)
    o_ref[...] = jnp.concatenate(outs, axis=-1).astype(o_ref.dtype)


def _attention(qkv, sinks, bias, bsz, seq):
    t = qkv.shape[0]
    nb = seq // BLOCK
    kcol = A_WIDTH // A_KV_WIDTH
    vcol = kcol + 1

    def cur(b, n):
        return b * nb + n

    def prev(b, n):
        return b * nb + jnp.maximum(n - 1, 0)

    return pl.pallas_call(
        _attn_kernel,
        grid=(bsz, nb),
        in_specs=[pl.BlockSpec(memory_space=pltpu.SMEM),
                  pl.BlockSpec((BLOCK, A_WIDTH), lambda b, n: (cur(b, n), 0)),
                  pl.BlockSpec((BLOCK, A_KV_WIDTH), lambda b, n: (prev(b, n), kcol)),
                  pl.BlockSpec((BLOCK, A_KV_WIDTH), lambda b, n: (cur(b, n), kcol)),
                  pl.BlockSpec((BLOCK, A_KV_WIDTH), lambda b, n: (prev(b, n), vcol)),
                  pl.BlockSpec((BLOCK, A_KV_WIDTH), lambda b, n: (cur(b, n), vcol)),
                  pl.BlockSpec((A_Q_HEADS, BLOCK, 2 * BLOCK), lambda b, n: (0, 0, 0))],
        out_specs=pl.BlockSpec((BLOCK, A_WIDTH), lambda b, n: (cur(b, n), 0)),
        out_shape=jax.ShapeDtypeStruct((t, A_WIDTH), BF16),
        compiler_params=_cparams("parallel", "parallel"),
        name="swa_attention",
    )(sinks, qkv, qkv, qkv, qkv, qkv, bias)


def _rwkv_prep_kernel(seq, zb_ref, halo_ref, mu_ref, w0_ref, wup_ref, a0_ref, aup_ref,
                      gup_ref, kk_ref, ka_ref, rk_ref, ones_ref,
                      p_ref, v_ref, bonus_ref, g_ref):
    tm = zb_ref.shape[0]
    i = pl.program_id(0)
    z = zb_ref[...]
    seq_start = (i * tm) % seq == 0
    prev_last = jnp.where(seq_start, 0.0, halo_ref[V7X_SUBLANES - 1:V7X_SUBLANES, :])
    row = lax.broadcasted_iota(jnp.int32, z.shape, 0)
    zsh = jnp.where(row == 0, prev_last, pltpu.roll(z, 1, axis=0))
    zs = z + (zsh - z) * mu_ref[...]
    c = B_WIDTH
    zr, zk, zv = zs[:, :c], zs[:, c:2 * c], zs[:, 2 * c:3 * c]
    zwd = zs[:, 3 * c:3 * c + DECAY_LORA]
    zad = zs[:, 3 * c + DECAY_LORA:3 * c + DECAY_LORA + AAA_LORA]
    zgd = zs[:, 3 * c + DECAY_LORA + AAA_LORA:]
    ones = ones_ref[...]

    w = -jax.nn.softplus(-(w0_ref[...] + _dot(jnp.tanh(zwd).astype(BF16), wup_ref[...]))) - 0.5
    decay = jnp.exp(-jnp.exp(w))
    a = jax.nn.sigmoid(a0_ref[...] + _dot(zad.astype(BF16), aup_ref[...]))
    g = _dot(jax.nn.sigmoid(zgd).astype(BF16), gup_ref[...])

    kk = zk * kk_ref[...]
    kk = kk / jnp.maximum(jnp.sqrt(_head_sum(kk * kk, ones)), 1e-12)
    k = zk * (1.0 + (a - 1.0) * ka_ref[...])
    p_ref[0] = zr
    p_ref[1] = decay
    p_ref[2] = k
    p_ref[3] = -kk
    p_ref[4] = kk * a
    v_ref[...] = zv
    bonus_ref[...] = _head_sum(zr * k * rk_ref[...], ones) * zv
    g_ref[...] = g


def _rwkv_prep(zb, seq, mu, w0, w_up, a0, a_up, g_up, k_k, k_a, r_k, ones):
    t, n = zb.shape
    tm = TOKEN_TILE
    hb = tm // V7X_SUBLANES
    c = B_WIDTH

    def row(v):
        return v.reshape(1, -1)

    def full(a):
        return pl.BlockSpec(a.shape, lambda i: (0,) * a.ndim)

    small = [row(mu), row(w0), w_up.astype(BF16), row(a0), a_up.astype(BF16),
             g_up.astype(BF16), row(k_k), row(k_a), row(r_k), ones]
    tok = pl.BlockSpec((tm, c), lambda i: (i, 0))
    return pl.pallas_call(
        functools.partial(_rwkv_prep_kernel, seq),
        grid=(t // tm,),
        in_specs=[pl.BlockSpec((tm, n), lambda i: (i, 0)),
                  pl.BlockSpec((V7X_SUBLANES, n), lambda i: (jnp.maximum(i * hb - 1, 0), 0))]
                 + [full(a) for a in small],
        out_specs=[pl.BlockSpec((5, tm, c), lambda i: (0, i, 0)), tok, tok, tok],
        out_shape=[jax.ShapeDtypeStruct((5, t, c), F32)]
                  + [jax.ShapeDtypeStruct((t, c), F32)] * 3,
        compiler_params=_cparams("parallel"),
        name="rwkv_prep",
    )(zb, zb, *small)


def _rwkv_scan_kernel(p_ref, v_ref, y_ref, s_ref):
    @pl.when(pl.program_id(0) == 0)
    def _():
        s_ref[...] = jnp.zeros_like(s_ref)

    def vec(t, which, j):
        return p_ref[t, which, pl.ds(j, 1), :]

    def step(t, carry):
        v_t = v_ref[t]
        sa = [jnp.zeros_like(v_t), jnp.zeros_like(v_t)]
        for j in range(HEAD_DIM):
            sa[j % 2] = sa[j % 2] + s_ref[j] * vec(t, 3, j)
        sa = sa[0] + sa[1]
        y = [jnp.zeros_like(v_t), jnp.zeros_like(v_t)]
        for j in range(HEAD_DIM):
            s = s_ref[j] * vec(t, 1, j) + sa * vec(t, 4, j) + v_t * vec(t, 2, j)
            s_ref[j] = s
            y[j % 2] = y[j % 2] + s * vec(t, 0, j)
        y_ref[t] = y[0] + y[1]
        return carry

    lax.fori_loop(0, p_ref.shape[0], step, 0)


def _rwkv_scan(p, v):
    seq = p.shape[0]
    tt = SCAN_STEPS
    half = HEAD_DIM // 2
    return pl.pallas_call(
        _rwkv_scan_kernel,
        grid=(seq // tt,),
        in_specs=[pl.BlockSpec((tt, 5, HEAD_DIM, V7X_LANES), lambda i: (i, 0, 0, 0)),
                  pl.BlockSpec((tt, half, V7X_LANES), lambda i: (i, 0, 0))],
        out_specs=pl.BlockSpec((tt, half, V7X_LANES), lambda i: (i, 0, 0)),
        out_shape=jax.ShapeDtypeStruct((seq, half, V7X_LANES), F32),
        scratch_shapes=[pltpu.VMEM((HEAD_DIM, half, V7X_LANES), F32)],
        compiler_params=_cparams("arbitrary"),
        name="rwkv_scan",
    )(p, v)


def _mix_out_kernel(ya_ref, y_ref, bonus_ref, g_ref, h_ref, gng_ref, gnb_ref, ones_ref,
                    wa_ref, wb_ref, lng_ref, lnb_ref, o_ref):
    ones = ones_ref[...]
    y = y_ref[...]
    inv = 1.0 / HEAD_DIM
    d = y - _head_sum(y, ones) * inv
    var = _head_sum(d * d, ones) * inv
    yn = d * lax.rsqrt(var + GN_EPS) * gng_ref[...] + gnb_ref[...]
    yb = (yn + bonus_ref[...]) * g_ref[...]
    mix = _dot(ya_ref[...], wa_ref[...]) + _dot(yb.astype(BF16), wb_ref[...])
    o_ref[...] = _layer_norm(ALPHA * h_ref[...] + mix, lng_ref[...], lnb_ref[...])


def _mix_out(ya, y, bonus, g, h, gn_g, gn_b, ones, w_out, ln_g, ln_b):
    t, d = h.shape
    tm = TOKEN_TILE
    c = B_WIDTH
    wa = w_out[:A_WIDTH].astype(BF16)
    wb = w_out[A_WIDTH:].astype(BF16)
    small = [gn_g.reshape(1, -1), gn_b.reshape(1, -1), ones, wa, wb,
             ln_g.reshape(1, -1), ln_b.reshape(1, -1)]
    tok = pl.BlockSpec((tm, c), lambda i: (i, 0))
    return pl.pallas_call(
        _mix_out_kernel,
        grid=(t // tm,),
        in_specs=[tok, tok, tok, tok, pl.BlockSpec((tm, d), lambda i: (i, 0))]
                 + [pl.BlockSpec(a.shape, lambda i: (0, 0)) for a in small],
        out_specs=pl.BlockSpec((tm, d), lambda i: (i, 0)),
        out_shape=jax.ShapeDtypeStruct((t, d), F32),
        compiler_params=_cparams("parallel"),
        name="mix_out",
    )(ya, y, bonus, g, h, *small)


def _ffn_kernel(x_ref, wg_ref, wu_ref, wd_ref, lng_ref, lnb_ref, o_ref, xb_ref, acc_ref):
    f = pl.program_id(1)

    @pl.when(f == 0)
    def _():
        xb_ref[...] = x_ref[...].astype(BF16)
        acc_ref[...] = jnp.zeros_like(acc_ref)

    xb = xb_ref[...]
    hid = jax.nn.silu(_dot(xb, wg_ref[...])) * _dot(xb, wu_ref[...])
    acc_ref[...] += _dot(hid.astype(BF16), wd_ref[...])

    @pl.when(f == pl.num_programs(1) - 1)
    def _():
        o_ref[...] = _layer_norm(ALPHA * x_ref[...] + acc_ref[...], lng_ref[...], lnb_ref[...])


def _ffn_chunk(d_ff):
    for tf in (512, 256, 128):
        if d_ff % tf == 0:
            return tf
    raise ValueError(f"d_ff={d_ff} is not a multiple of {V7X_LANES}")


def _dense_ffn(x, w_gate, w_up, w_down, ln_g, ln_b):
    t, d = x.shape
    d_ff = w_gate.shape[1]
    tm = TOKEN_TILE
    tf = _ffn_chunk(d_ff)
    return pl.pallas_call(
        _ffn_kernel,
        grid=(t // tm, d_ff // tf),
        in_specs=[pl.BlockSpec((tm, d), lambda i, f: (i, 0)),
                  pl.BlockSpec((d, tf), lambda i, f: (0, f)),
                  pl.BlockSpec((d, tf), lambda i, f: (0, f)),
                  pl.BlockSpec((tf, d), lambda i, f: (f, 0)),
                  pl.BlockSpec((1, d), lambda i, f: (0, 0)),
                  pl.BlockSpec((1, d), lambda i, f: (0, 0))],
        out_specs=pl.BlockSpec((tm, d), lambda i, f: (i, 0)),
        out_shape=jax.ShapeDtypeStruct((t, d), F32),
        scratch_shapes=[pltpu.VMEM((tm, d), BF16), pltpu.VMEM((tm, d), F32)],
        compiler_params=_cparams("parallel", "arbitrary"),
        name="dense_ffn",
    )(x, w_gate.astype(BF16), w_up.astype(BF16), w_down.astype(BF16),
      ln_g.reshape(1, -1), ln_b.reshape(1, -1))


def _conv_mixer_kernel(seq, x_ref, halo_ref, win_ref, cw_ref, wout_ref, lng_ref, lnb_ref, o_ref):
    tm, d = x_ref.shape
    i = pl.program_id(0)
    x = x_ref[...]
    seq_start = (i * tm) % seq == 0
    xa = jnp.concatenate([halo_ref[...], x], axis=0).astype(BF16)
    z = _dot(xa, win_ref[...])
    gb = z[V7X_SUBLANES:, :d]
    u = z[:, d:2 * d] * z[:, 2 * d:]
    row = lax.broadcasted_iota(jnp.int32, u.shape, 0)
    u = jnp.where((row < V7X_SUBLANES) & seq_start, 0.0, u)
    cw = cw_ref[...]
    conv = pltpu.roll(u, 2, axis=0)[V7X_SUBLANES:] * cw[0:1]
    conv = conv + pltpu.roll(u, 1, axis=0)[V7X_SUBLANES:] * cw[1:2]
    conv = conv + u[V7X_SUBLANES:] * cw[2:3]
    y = _dot((gb * conv).astype(BF16), wout_ref[...])
    o_ref[...] = _layer_norm(ALPHA * x + y, lng_ref[...], lnb_ref[...])


def _conv_mixer(x, seq, w_in, conv_w, w_out, ln_g, ln_b):
    t, d = x.shape
    tm = TOKEN_TILE
    hb = tm // V7X_SUBLANES
    small = [w_in.astype(BF16), conv_w, w_out.astype(BF16), ln_g.reshape(1, -1), ln_b.reshape(1, -1)]
    return pl.pallas_call(
        functools.partial(_conv_mixer_kernel, seq),
        grid=(t // tm,),
        in_specs=[pl.BlockSpec((tm, d), lambda i: (i, 0)),
                  pl.BlockSpec((V7X_SUBLANES, d), lambda i: (jnp.maximum(i * hb - 1, 0), 0))]
                 + [pl.BlockSpec(a.shape, lambda i: (0, 0)) for a in small],
        out_specs=pl.BlockSpec((tm, d), lambda i: (i, 0)),
        out_shape=jax.ShapeDtypeStruct((t, d), F32),
        compiler_params=_cparams("parallel"),
        name="conv_mixer",
    )(x, x, *small)


def _router_kernel(x_ref, w_ref, idx_ref, prob_ref):
    logits = jnp.dot(x_ref[...], w_ref[...], preferred_element_type=F32,
                     precision=lax.Precision.HIGHEST)
    lane = lax.broadcasted_iota(jnp.int32, logits.shape, 1)
    logits = jnp.where(lane < N_EXPERTS, logits, -jnp.inf)
    m1 = jnp.max(logits, axis=-1, keepdims=True)
    i1 = jnp.min(jnp.where(logits == m1, lane, ROUTER_LANES), axis=-1, keepdims=True)
    rest = jnp.where(lane == i1, -jnp.inf, logits)
    m2 = jnp.max(rest, axis=-1, keepdims=True)
    i2 = jnp.min(jnp.where(rest == m2, lane, ROUTER_LANES), axis=-1, keepdims=True)
    e2 = jnp.exp(m2 - m1)
    den = 1.0 + e2
    idx_ref[...] = jnp.where(lane == 0, i1, jnp.where(lane == 1, i2, 0))
    prob_ref[...] = jnp.where(lane == 0, 1.0 / den, jnp.where(lane == 1, e2 / den, 0.0))


def _router(x, router_w):
    t, d = x.shape
    tm = TOKEN_TILE
    w = jnp.zeros((d, ROUTER_LANES), F32).at[:, :N_EXPERTS].set(router_w)
    out = pl.BlockSpec((tm, ROUTER_LANES), lambda i: (i, 0))
    idx, prob = pl.pallas_call(
        _router_kernel,
        grid=(t // tm,),
        in_specs=[pl.BlockSpec((tm, d), lambda i: (i, 0)),
                  pl.BlockSpec((d, ROUTER_LANES), lambda i: (0, 0))],
        out_specs=[out, out],
        out_shape=[jax.ShapeDtypeStruct((t, ROUTER_LANES), jnp.int32),
                   jax.ShapeDtypeStruct((t, ROUTER_LANES), F32)],
        compiler_params=_cparams("parallel"),
        name="router",
    )(x, w)
    return idx[:, :TOP_K], prob


def _dispatch_kernel(cnt_ref, start_ref, end_ref, pos_ref, h_ref, xs_ref, buf_ref, zero_ref, sem):
    tm = h_ref.shape[0]
    rows = xs_ref.shape[0]
    for s in range(V7X_SUBLANES):
        buf_ref[:, s, :] = h_ref[:, s * V7X_LANES:(s + 1) * V7X_LANES]

    def row_copy(r, k):
        return pltpu.make_async_copy(buf_ref.at[r], xs_ref.at[pos_ref[0, k, r]], sem)

    def zero_copy(r):
        return pltpu.make_async_copy(zero_ref.at[0], xs_ref.at[r], sem)

    def for_each_padding_row(fn):
        for e in range(N_EXPERTS):
            lax.fori_loop(start_ref[e] + cnt_ref[e], end_ref[e], fn, 0)
        lax.fori_loop(end_ref[N_EXPERTS - 1], rows, fn, 0)

    def start(r, c):
        row_copy(r, 0).start()
        row_copy(r, 1).start()
        return c

    def wait(r, c):
        row_copy(r, 0).wait()
        row_copy(r, 1).wait()
        return c

    def zero_start(r, c):
        zero_copy(r).start()
        return c

    def zero_wait(r, c):
        zero_copy(r).wait()
        return c

    lax.fori_loop(0, tm, start, 0)

    @pl.when(pl.program_id(0) == 0)
    def _():
        zero_ref[...] = jnp.zeros_like(zero_ref)
        for_each_padding_row(zero_start)
        for_each_padding_row(zero_wait)

    lax.fori_loop(0, tm, wait, 0)


def _moe_dispatch(h, pos_tiles, counts, starts, ends, rows):
    t, d = h.shape
    tm = pos_tiles.shape[2]
    s = d // V7X_LANES
    return pl.pallas_call(
        _dispatch_kernel,
        grid_spec=pltpu.PrefetchScalarGridSpec(
            num_scalar_prefetch=3,
            grid=(t // tm,),
            in_specs=[pl.BlockSpec((1, TOP_K, tm), lambda i, *_: (i, 0, 0), memory_space=pltpu.SMEM),
                      pl.BlockSpec((tm, d), lambda i, *_: (i, 0))],
            out_specs=pl.BlockSpec(memory_space=pl.ANY),
            scratch_shapes=[pltpu.VMEM((tm, s, V7X_LANES), F32),
                            pltpu.VMEM((1, s, V7X_LANES), F32),
                            pltpu.SemaphoreType.DMA(())]),
        out_shape=jax.ShapeDtypeStruct((rows, s, V7X_LANES), F32),
        compiler_params=_cparams("arbitrary"),
        name="moe_dispatch",
    )(counts, starts, ends, pos_tiles, h)


def _moe_kernel(te_ref, nv_ref, x_ref, wg_ref, wu_ref, wd_ref, o_ref, xb_ref, acc_ref):
    i = pl.program_id(0)
    f = pl.program_id(1)
    last = pl.num_programs(1) - 1
    valid = i < nv_ref[0]
    n_sub = x_ref.shape[1]

    @pl.when(valid & (f == 0))
    def _():
        x = jnp.concatenate([x_ref[:, s, :] for s in range(n_sub)], axis=-1)
        xb_ref[...] = x.astype(BF16)
        acc_ref[...] = jnp.zeros_like(acc_ref)

    @pl.when(valid)
    def _():
        x = xb_ref[...]
        hid = jax.nn.silu(_dot(x, wg_ref[0])) * _dot(x, wu_ref[0])
        acc_ref[...] += _dot(hid.astype(BF16), wd_ref[0])

    @pl.when(valid & (f == last))
    def _():
        for s in range(n_sub):
            o_ref[:, s, :] = acc_ref[:, s * V7X_LANES:(s + 1) * V7X_LANES]

    @pl.when(jnp.logical_not(valid) & (f == last))
    def _():
        o_ref[...] = jnp.zeros_like(o_ref)


def _moe_experts(xs, tile_expert, n_valid, w_gate, w_up, w_down):
    r, n_sub, lanes = xs.shape
    d = n_sub * lanes
    d_ff = w_gate.shape[2]
    tg = MOE_ROW_TILE
    tf = _ffn_chunk(d_ff)
    nf = d_ff // tf

    def f_eff(i, f, nv):
        return jnp.where(i < nv[0], f, nf - 1)

    row_tile = pl.BlockSpec((tg, n_sub, lanes), lambda i, f, te, nv: (i, 0, 0))
    return pl.pallas_call(
        _moe_kernel,
        grid_spec=pltpu.PrefetchScalarGridSpec(
            num_scalar_prefetch=2,
            grid=(r // tg, nf),
            in_specs=[row_tile,
                      pl.BlockSpec((1, d, tf), lambda i, f, te, nv: (te[i], 0, f_eff(i, f, nv))),
                      pl.BlockSpec((1, d, tf), lambda i, f, te, nv: (te[i], 0, f_eff(i, f, nv))),
                      pl.BlockSpec((1, tf, d), lambda i, f, te, nv: (te[i], f_eff(i, f, nv), 0))],
            out_specs=row_tile,
            scratch_shapes=[pltpu.VMEM((tg, d), BF16), pltpu.VMEM((tg, d), F32)]),
        out_shape=jax.ShapeDtypeStruct((r, n_sub, lanes), F32),
        compiler_params=_cparams("parallel", "arbitrary"),
        name="moe_experts",
    )(tile_expert, n_valid, xs, w_gate, w_up, w_down)


def _combine_kernel(pos_ref, h_ref, p_ref, y_ref, lng_ref, lnb_ref, o_ref, buf_ref, sem):
    tm = h_ref.shape[0]
    n_sub = buf_ref.shape[2]

    def row_copy(r, k):
        return pltpu.make_async_copy(y_ref.at[pos_ref[0, k, r]], buf_ref.at[k, r], sem)

    def start(r, c):
        row_copy(r, 0).start()
        row_copy(r, 1).start()
        return c

    def wait(r, c):
        row_copy(r, 0).wait()
        row_copy(r, 1).wait()
        return c

    lax.fori_loop(0, tm, start, 0)
    lax.fori_loop(0, tm, wait, 0)
    p = p_ref[...]
    p0, p1 = p[:, 0:1], p[:, 1:2]
    ffn = jnp.concatenate([p0 * buf_ref[0, :, s, :] + p1 * buf_ref[1, :, s, :]
                           for s in range(n_sub)], axis=-1)
    o_ref[...] = _layer_norm(ALPHA * h_ref[...] + ffn, lng_ref[...], lnb_ref[...])


def _moe_combine(h, top_p, y, pos_tiles, ln_g, ln_b):
    t, d = h.shape
    tm = pos_tiles.shape[2]
    n_sub, lanes = y.shape[1:]
    return pl.pallas_call(
        _combine_kernel,
        grid=(t // tm,),
        in_specs=[pl.BlockSpec((1, TOP_K, tm), lambda i: (i, 0, 0), memory_space=pltpu.SMEM),
                  pl.BlockSpec((tm, d), lambda i: (i, 0)),
                  pl.BlockSpec((tm, ROUTER_LANES), lambda i: (i, 0)),
                  pl.BlockSpec(memory_space=pl.ANY),
                  pl.BlockSpec((1, d), lambda i: (0, 0)),
                  pl.BlockSpec((1, d), lambda i: (0, 0))],
        out_specs=pl.BlockSpec((tm, d), lambda i: (i, 0)),
        out_shape=jax.ShapeDtypeStruct((t, d), F32),
        scratch_shapes=[pltpu.VMEM((TOP_K, tm, n_sub, lanes), F32), pltpu.SemaphoreType.DMA(())],
        compiler_params=_cparams("arbitrary"),
        name="moe_combine",
    )(pos_tiles, h, top_p, y, ln_g.reshape(1, -1), ln_b.reshape(1, -1))


def _route_plan(top_idx, t):
    tg = MOE_ROW_TILE
    e_flat = top_idx.T.reshape(-1)
    onehot = (e_flat[:, None] == jnp.arange(N_EXPERTS)[None, :]).astype(jnp.int32)
    csum = jnp.cumsum(onehot, axis=0)
    counts = csum[-1]
    padded = (counts + tg - 1) // tg * tg
    ends = jnp.cumsum(padded)
    starts = ends - padded
    pos = jnp.sum(onehot * (csum - 1 + starts[None, :]), axis=1)
    rows = TOP_K * t + N_EXPERTS * tg
    tile_start = jnp.arange(rows // tg, dtype=jnp.int32) * tg
    tile_expert = jnp.sum((ends[None, :] <= tile_start[:, None]).astype(jnp.int32), axis=1)
    tile_expert = jnp.minimum(tile_expert, N_EXPERTS - 1)
    n_valid = (ends[N_EXPERTS - 1] // tg).reshape(1)
    tm = GATHER_TILE
    pos_tiles = jnp.transpose(pos.reshape(TOP_K, t // tm, tm), (1, 0, 2))
    return pos_tiles, counts, starts, ends, tile_expert, n_valid, rows


def _moe(h, router_w, w_gate, w_up, w_down, ln_g, ln_b):
    top_idx, top_p = _router(h, router_w)
    pos_tiles, counts, starts, ends, tile_expert, n_valid, rows = _route_plan(top_idx, h.shape[0])
    xs = _moe_dispatch(h, pos_tiles, counts, starts, ends, rows)
    y = _moe_experts(xs, tile_expert, n_valid,
                     w_gate.astype(BF16), w_up.astype(BF16), w_down.astype(BF16))
    return _moe_combine(h, top_p, y, pos_tiles, ln_g, ln_b)


def _to_scan_layout(p, v, bsz, seq):
    half = HEAD_DIM // 2
    p = p.reshape(5, bsz, seq, B_HEADS, HEAD_DIM)
    p = jnp.transpose(p, (2, 0, 4, 1, 3)).reshape(seq, 5, HEAD_DIM, bsz * B_HEADS)
    p = jnp.concatenate([p, p], axis=-1)
    v = v.reshape(bsz, seq, B_HEADS, 2, half)
    v = jnp.transpose(v, (1, 4, 3, 0, 2)).reshape(seq, half, 2 * bsz * B_HEADS)
    return p, v


def _from_scan_layout(y, bsz, seq):
    half = HEAD_DIM // 2
    y = y.reshape(seq, half, 2, bsz, B_HEADS)
    return jnp.transpose(y, (3, 0, 4, 2, 1)).reshape(bsz * seq, B_WIDTH)


def kernel(x, rel_bias_table, even_w_in, even_sinks, rwkv_mu, rwkv_w0, rwkv_w_up, rwkv_a0, rwkv_a_up, rwkv_g_up, rwkv_k_k, rwkv_k_a, rwkv_r_k, rwkv_gn_g, rwkv_gn_b, even_w_out, even_ln_mix_g, even_ln_mix_b, dense_w_gate, dense_w_up, dense_w_down, even_ln_ffn_g, even_ln_ffn_b, odd_w_in, odd_conv_w, odd_w_out, odd_ln_mix_g, odd_ln_mix_b, router_w, moe_w_gate, moe_w_up, moe_w_down, odd_ln_ffn_g, odd_ln_ffn_b):
    bsz, seq, d = x.shape
    assert seq % TOKEN_TILE == 0 and seq % BLOCK == 0 and seq % SCAN_STEPS == 0
    assert bsz * B_HEADS * 2 == V7X_LANES, "scan layout puts (half, batch, head) on the lanes"
    h = x.reshape(bsz * seq, d)
    ones = jnp.asarray(np.kron(np.eye(B_HEADS), np.ones((HEAD_DIM, HEAD_DIM))), BF16)
    bias = _rel_bias(rel_bias_table)
    for layer in range(DEPTH):
        i = layer // 2
        if layer % 2 == 0:
            qkv, zb = _in_proj(h, even_w_in[i].astype(BF16))
            ya = _attention(qkv, even_sinks[i], bias, bsz, seq)
            p, v, bonus, g = _rwkv_prep(zb, seq, rwkv_mu[i], rwkv_w0[i], rwkv_w_up[i], rwkv_a0[i],
                                        rwkv_a_up[i], rwkv_g_up[i], rwkv_k_k[i], rwkv_k_a[i],
                                        rwkv_r_k[i], ones)
            y = _from_scan_layout(_rwkv_scan(*_to_scan_layout(p, v, bsz, seq)), bsz, seq)
            h = _mix_out(ya, y, bonus, g, h, rwkv_gn_g[i], rwkv_gn_b[i], ones, even_w_out[i],
                         even_ln_mix_g[i], even_ln_mix_b[i])
            h = _dense_ffn(h, dense_w_gate[i], dense_w_up[i], dense_w_down[i],
                           even_ln_ffn_g[i], even_ln_ffn_b[i])
        else:
            h = _conv_mixer(h, seq, odd_w_in[i], odd_conv_w[i], odd_w_out[i],
                            odd_ln_mix_g[i], odd_ln_mix_b[i])
            h = _moe(h, router_w[i], moe_w_gate[i], moe_w_up[i], moe_w_down[i],
                     odd_ln_ffn_g[i], odd_ln_ffn_b[i])
    return h.reshape(bsz, seq, d)
```

```python
import functools
import math

import numpy as np
import jax
import jax.numpy as jnp
from jax import lax
from jax.experimental import pallas as pl
from jax.experimental.pallas import tpu as pltpu

F32 = jnp.float32
BF16 = jnp.bfloat16

HEAD_DIM = 64
A_Q_HEADS = 8
A_KV_HEADS = 2
A_GROUP = A_Q_HEADS // A_KV_HEADS
WINDOW = 128
BLOCK = 128
NUM_BUCKETS = 32
MAX_DISTANCE = 128
B_HEADS = 8
A_WIDTH = A_Q_HEADS * HEAD_DIM
A_KV_WIDTH = A_KV_HEADS * HEAD_DIM
QKV_WIDTH = A_WIDTH + 2 * A_KV_WIDTH
B_WIDTH = B_HEADS * HEAD_DIM
DECAY_LORA = 64
AAA_LORA = 64
GATE_LORA = 128
B_IN_WIDTH = 3 * B_WIDTH + DECAY_LORA + AAA_LORA + GATE_LORA
CONV_WIDTH = 3
N_EXPERTS = 8
TOP_K = 2
DEPTH = 2
ALPHA = (2.0 * DEPTH) ** 0.25
LN_EPS = 1e-5
GN_EPS = 64e-5
NEG_INF = -1e30
ATTN_SCALE = HEAD_DIM ** -0.5

V7X_LANES = 128
V7X_SUBLANES = 8
V7X_MXU_DIM = 256
V7X_VMEM_LIMIT = 56 * 1024 * 1024
DENSE_FF_CHUNK = 4096
MOE_FF_CHUNK = 1792

TOKEN_TILE = 512
SCAN_CHUNK = 128
MOE_ROW_TILE = 512
GATHER_TILE = 256
ROUTER_LANES = 128
SLOT_ROWS = 9


def _cparams(*sem):
    return pltpu.CompilerParams(dimension_semantics=sem, vmem_limit_bytes=V7X_VMEM_LIMIT)


def _layer_norm(y, g, b):
    mu = jnp.mean(y, axis=-1, keepdims=True)
    d = y - mu
    var = jnp.mean(d * d, axis=-1, keepdims=True)
    return d * lax.rsqrt(var + LN_EPS) * g + b


def _dot(a, b):
    return jnp.dot(a, b, preferred_element_type=F32)


def _head_sum(x, ones_blockdiag):
    hi = x.astype(BF16)
    lo = (x - hi.astype(F32)).astype(BF16)
    return _dot(hi, ones_blockdiag) + _dot(lo, ones_blockdiag)


def _in_proj_kernel(x_ref, w_ref, qkv_ref, zb_ref):
    z = _dot(x_ref[...].astype(BF16), w_ref[...])
    qkv_ref[...] = z[:, :QKV_WIDTH].astype(BF16)
    zb_ref[...] = z[:, QKV_WIDTH:]


def _in_proj(h, w):
    t, d = h.shape
    n = w.shape[1]
    tm = TOKEN_TILE
    return pl.pallas_call(
        _in_proj_kernel,
        grid=(t // tm,),
        in_specs=[pl.BlockSpec((tm, d), lambda i: (i, 0)),
                  pl.BlockSpec((d, n), lambda i: (0, 0))],
        out_specs=[pl.BlockSpec((tm, QKV_WIDTH), lambda i: (i, 0)),
                   pl.BlockSpec((tm, n - QKV_WIDTH), lambda i: (i, 0))],
        out_shape=[jax.ShapeDtypeStruct((t, QKV_WIDTH), BF16),
                   jax.ShapeDtypeStruct((t, n - QKV_WIDTH), F32)],
        compiler_params=_cparams("parallel"),
        name="in_proj",
    )(h, w)


def _bucket_table():
    qi = np.arange(BLOCK)[:, None]
    ki = np.arange(2 * BLOCK)[None, :]
    n = np.maximum(qi + BLOCK - ki, 0)
    max_exact = NUM_BUCKETS // 2
    log_ratio = (np.log(np.maximum(n, 1).astype(np.float32) / max_exact)
                 / math.log(MAX_DISTANCE / max_exact))
    large = max_exact + (log_ratio * (NUM_BUCKETS - max_exact)).astype(np.int32)
    large = np.minimum(large, NUM_BUCKETS - 1)
    return np.where(n < max_exact, n, large).astype(np.int32)


def _bias_kernel(tab_ref, bucket_ref, out_ref):
    bucket = bucket_ref[...]
    for h in range(A_Q_HEADS):
        acc = jnp.zeros(bucket.shape, F32)
        for b in range(NUM_BUCKETS):
            acc = jnp.where(bucket == b, tab_ref[b, h], acc)
        out_ref[h] = acc


def _rel_bias(table):
    return pl.pallas_call(
        _bias_kernel,
        in_specs=[pl.BlockSpec(memory_space=pltpu.SMEM),
                  pl.BlockSpec(memory_space=pltpu.VMEM)],
        out_specs=pl.BlockSpec(memory_space=pltpu.VMEM),
        out_shape=jax.ShapeDtypeStruct((A_Q_HEADS, BLOCK, 2 * BLOCK), F32),
        name="rel_bias",
    )(table, jnp.asarray(_bucket_table()))


def _attn_kernel(sink_ref, q_ref, kp_ref, kc_ref, vp_ref, vc_ref, bias_ref, o_ref):
    n = pl.program_id(1)
    qi = lax.broadcasted_iota(jnp.int32, (BLOCK, 2 * BLOCK), 0)
    ki = lax.broadcasted_iota(jnp.int32, (BLOCK, 2 * BLOCK), 1)
    dist = qi + BLOCK - ki
    valid = (dist >= 0) & (dist < WINDOW) & ((ki >= BLOCK) | (n > 0))
    q = q_ref[...]
    kcat = jnp.concatenate([kp_ref[...], kc_ref[...]], axis=0)
    vcat = jnp.concatenate([vp_ref[...], vc_ref[...]], axis=0)
    outs = []
    for hk in range(A_KV_HEADS):
        k_h = kcat[:, hk * HEAD_DIM:(hk + 1) * HEAD_DIM]
        v_h = vcat[:, hk * HEAD_DIM:(hk + 1) * HEAD_DIM]
        for g in range(A_GROUP):
            hq = hk * A_GROUP + g
            q_h = q[:, hq * HEAD_DIM:(hq + 1) * HEAD_DIM]
            s = lax.dot_general(q_h, k_h, (((1,), (1,)), ((), ())),
                                preferred_element_type=F32)
            s = s * ATTN_SCALE + bias_ref[hq]
            s = jnp.where(valid, s, NEG_INF)
            sink = sink_ref[hq]
            m = jnp.maximum(jnp.max(s, axis=-1, keepdims=True), sink)
            p = jnp.exp(s - m)
            denom = jnp.sum(p, axis=-1, keepdims=True) + jnp.exp(sink - m)
            o = _dot(p.astype(BF16), v_h)
            outs.append(o / denom)
    o_ref[...] = jnp.concatenate(outs, axis=-1).astype(o_ref.dtype)


def _attention(qkv, sinks, bias, bsz, seq):
    t = qkv.shape[0]
    nb = seq // BLOCK
    kcol = A_WIDTH // A_KV_WIDTH
    vcol = kcol + 1

    def cur(b, n):
        return b * nb + n

    def prev(b, n):
        return b * nb + jnp.maximum(n - 1, 0)

    return pl.pallas_call(
        _attn_kernel,
        grid=(bsz, nb),
        in_specs=[pl.BlockSpec(memory_space=pltpu.SMEM),
                  pl.BlockSpec((BLOCK, A_WIDTH), lambda b, n: (cur(b, n), 0)),
                  pl.BlockSpec((BLOCK, A_KV_WIDTH), lambda b, n: (prev(b, n), kcol)),
                  pl.BlockSpec((BLOCK, A_KV_WIDTH), lambda b, n: (cur(b, n), kcol)),
                  pl.BlockSpec((BLOCK, A_KV_WIDTH), lambda b, n: (prev(b, n), vcol)),
                  pl.BlockSpec((BLOCK, A_KV_WIDTH), lambda b, n: (cur(b, n), vcol)),
                  pl.BlockSpec((A_Q_HEADS, BLOCK, 2 * BLOCK), lambda b, n: (0, 0, 0))],
        out_specs=pl.BlockSpec((BLOCK, A_WIDTH), lambda b, n: (cur(b, n), 0)),
        out_shape=jax.ShapeDtypeStruct((t, A_WIDTH), BF16),
        compiler_params=_cparams("parallel", "parallel"),
        name="swa_attention",
    )(sinks, qkv, qkv, qkv, qkv, qkv, bias)


def _rwkv_prep_kernel(seq, zb_ref, halo_ref, mu_ref, w0_ref, wup_ref, a0_ref, aup_ref,
                      gup_ref, kk_ref, ka_ref, rk_ref, ones_ref,
                      p_ref, v_ref, bonus_ref, g_ref):
    tm = zb_ref.shape[0]
    i = pl.program_id(0)
    z = zb_ref[...]
    seq_start = (i * tm) % seq == 0
    prev_last = jnp.where(seq_start, 0.0, halo_ref[V7X_SUBLANES - 1:V7X_SUBLANES, :])
    row = lax.broadcasted_iota(jnp.int32, z.shape, 0)
    zsh = jnp.where(row == 0, prev_last, pltpu.roll(z, 1, axis=0))
    zs = z + (zsh - z) * mu_ref[...]
    c = B_WIDTH
    zr, zk, zv = zs[:, :c], zs[:, c:2 * c], zs[:, 2 * c:3 * c]
    zwd = zs[:, 3 * c:3 * c + DECAY_LORA]
    zad = zs[:, 3 * c + DECAY_LORA:3 * c + DECAY_LORA + AAA_LORA]
    zgd = zs[:, 3 * c + DECAY_LORA + AAA_LORA:]
    ones = ones_ref[...]

    w = -jax.nn.softplus(-(w0_ref[...] + _dot(jnp.tanh(zwd).astype(BF16), wup_ref[...]))) - 0.5
    decay = jnp.exp(-jnp.exp(w))
    a = jax.nn.sigmoid(a0_ref[...] + _dot(zad.astype(BF16), aup_ref[...]))
    g = _dot(jax.nn.sigmoid(zgd).astype(BF16), gup_ref[...])

    kk = zk * kk_ref[...]
    kk = kk / jnp.maximum(jnp.sqrt(_head_sum(kk * kk, ones)), 1e-12)
    k = zk * (1.0 + (a - 1.0) * ka_ref[...])
    p_ref[0] = zr.T
    p_ref[1] = decay.T
    p_ref[2] = k.T
    p_ref[3] = (-kk).T
    p_ref[4] = (kk * a).T
    v_ref[...] = zv
    bonus_ref[...] = _head_sum(zr * k * rk_ref[...], ones) * zv
    g_ref[...] = g


def _rwkv_prep(zb, seq, mu, w0, w_up, a0, a_up, g_up, k_k, k_a, r_k, ones):
    t, n = zb.shape
    tm = TOKEN_TILE
    hb = tm // V7X_SUBLANES
    c = B_WIDTH

    def row(v):
        return v.reshape(1, -1)

    def full(a):
        return pl.BlockSpec(a.shape, lambda i: (0,) * a.ndim)

    small = [row(mu), row(w0), w_up.astype(BF16), row(a0), a_up.astype(BF16),
             g_up.astype(BF16), row(k_k), row(k_a), row(r_k), ones]
    tok = pl.BlockSpec((tm, c), lambda i: (i, 0))
    return pl.pallas_call(
        functools.partial(_rwkv_prep_kernel, seq),
        grid=(t // tm,),
        in_specs=[pl.BlockSpec((tm, n), lambda i: (i, 0)),
                  pl.BlockSpec((V7X_SUBLANES, n), lambda i: (jnp.maximum(i * hb - 1, 0), 0))]
                 + [full(a) for a in small],
        out_specs=[pl.BlockSpec((5, c, tm), lambda i: (0, 0, i)), tok, tok, tok],
        out_shape=[jax.ShapeDtypeStruct((5, c, t), F32)]
                  + [jax.ShapeDtypeStruct((t, c), F32)] * 3,
        compiler_params=_cparams("parallel"),
        name="rwkv_prep",
    )(zb, zb, *small)


def _rwkv_scan_kernel(*refs):
    n_b = len(refs) - 4
    p_refs = refs[:n_b]
    v_ref, y_ref, s_ref, p2_ref = refs[n_b:]

    @pl.when(pl.program_id(0) == 0)
    def _():
        s_ref[...] = jnp.zeros_like(s_ref)

    def relayout(j, carry):
        r0 = pl.multiple_of(j * B_HEADS, B_HEADS)
        for which in range(p2_ref.shape[0]):
            x = jnp.concatenate([p_refs[b][which, pl.ds(r0, B_HEADS), :] for b in range(n_b)] * 2,
                                axis=0)
            p2_ref[which, j] = x.T
        return carry

    lax.fori_loop(0, HEAD_DIM, relayout, 0)

    def vec(t, which, j):
        return p2_ref[which, j, pl.ds(t, 1), :]

    n_steps = v_ref.shape[0]
    zero = jnp.zeros(v_ref.shape[1:], F32)

    sa_first = [zero, zero]
    for j in range(HEAD_DIM):
        sa_first[j % 2] = sa_first[j % 2] + s_ref[j] * vec(0, 3, j)

    def step(t, sa):
        v_t = v_ref[t]
        t_next = jnp.minimum(t + 1, n_steps - 1)
        y = [zero, zero]
        sa_next = [zero, zero]
        for j in range(HEAD_DIM):
            s = s_ref[j] * vec(t, 1, j) + sa * vec(t, 4, j) + v_t * vec(t, 2, j)
            s_ref[j] = s
            y[j % 2] = y[j % 2] + s * vec(t, 0, j)
            sa_next[j % 2] = sa_next[j % 2] + s * vec(t_next, 3, j)
        y_ref[t] = y[0] + y[1]
        return sa_next[0] + sa_next[1]

    lax.fori_loop(0, n_steps, step, sa_first[0] + sa_first[1])


def _rwkv_scan(p, v, bsz):
    n_vec, c, t = p.shape
    seq = t // bsz
    tt = SCAN_CHUNK
    nc = seq // tt
    half = HEAD_DIM // 2
    p_specs = [pl.BlockSpec((n_vec, c, tt), lambda i, b=b: (0, 0, b * nc + i),
                            pipeline_mode=pl.Buffered(1)) for b in range(bsz)]
    return pl.pallas_call(
        _rwkv_scan_kernel,
        grid=(nc,),
        in_specs=p_specs + [pl.BlockSpec((tt, half, V7X_LANES), lambda i: (i, 0, 0))],
        out_specs=pl.BlockSpec((tt, half, V7X_LANES), lambda i: (i, 0, 0)),
        out_shape=jax.ShapeDtypeStruct((seq, half, V7X_LANES), F32),
        scratch_shapes=[pltpu.VMEM((HEAD_DIM, half, V7X_LANES), F32),
                        pltpu.VMEM((n_vec, HEAD_DIM, tt, V7X_LANES), F32)],
        compiler_params=_cparams("arbitrary"),
        name="rwkv_scan",
    )(*([p] * bsz), v)


def _mix_out_kernel(ya_ref, y_ref, bonus_ref, g_ref, h_ref, gng_ref, gnb_ref, ones_ref,
                    wa_ref, wb_ref, lng_ref, lnb_ref, o_ref):
    ones = ones_ref[...]
    y = y_ref[...]
    inv = 1.0 / HEAD_DIM
    d = y - _head_sum(y, ones) * inv
    var = _head_sum(d * d, ones) * inv
    yn = d * lax.rsqrt(var + GN_EPS) * gng_ref[...] + gnb_ref[...]
    yb = (yn + bonus_ref[...]) * g_ref[...]
    mix = _dot(ya_ref[...], wa_ref[...]) + _dot(yb.astype(BF16), wb_ref[...])
    o_ref[...] = _layer_norm(ALPHA * h_ref[...] + mix, lng_ref[...], lnb_ref[...])


def _mix_out(ya, y, bonus, g, h, gn_g, gn_b, ones, w_out, ln_g, ln_b):
    t, d = h.shape
    tm = TOKEN_TILE
    c = B_WIDTH
    wa = w_out[:A_WIDTH].astype(BF16)
    wb = w_out[A_WIDTH:].astype(BF16)
    small = [gn_g.reshape(1, -1), gn_b.reshape(1, -1), ones, wa, wb,
             ln_g.reshape(1, -1), ln_b.reshape(1, -1)]
    tok = pl.BlockSpec((tm, c), lambda i: (i, 0))
    return pl.pallas_call(
        _mix_out_kernel,
        grid=(t // tm,),
        in_specs=[tok, tok, tok, tok, pl.BlockSpec((tm, d), lambda i: (i, 0))]
                 + [pl.BlockSpec(a.shape, lambda i: (0, 0)) for a in small],
        out_specs=pl.BlockSpec((tm, d), lambda i: (i, 0)),
        out_shape=jax.ShapeDtypeStruct((t, d), F32),
        compiler_params=_cparams("parallel"),
        name="mix_out",
    )(ya, y, bonus, g, h, *small)


def _ffn_kernel(x_ref, wg_ref, wu_ref, wd_ref, lng_ref, lnb_ref, o_ref, xb_ref, acc_ref):
    f = pl.program_id(1)

    @pl.when(f == 0)
    def _():
        xb_ref[...] = x_ref[...].astype(BF16)
        acc_ref[...] = jnp.zeros_like(acc_ref)

    xb = xb_ref[...]
    hid = jax.nn.silu(_dot(xb, wg_ref[...])) * _dot(xb, wu_ref[...])
    acc_ref[...] += _dot(hid.astype(BF16), wd_ref[...])

    @pl.when(f == pl.num_programs(1) - 1)
    def _():
        o_ref[...] = _layer_norm(ALPHA * x_ref[...] + acc_ref[...], lng_ref[...], lnb_ref[...])


def _ffn_chunk(d_ff, limit):
    for unit in (V7X_MXU_DIM, V7X_LANES):
        for n in range(limit // unit, 0, -1):
            if d_ff % (n * unit) == 0:
                return n * unit
    raise ValueError(f"d_ff={d_ff} is not a multiple of {V7X_LANES}")


def _dense_ffn(x, w_gate, w_up, w_down, ln_g, ln_b):
    t, d = x.shape
    d_ff = w_gate.shape[1]
    tm = TOKEN_TILE
    tf = _ffn_chunk(d_ff, DENSE_FF_CHUNK)
    mode = dict(pipeline_mode=pl.Buffered(1)) if tf == d_ff else {}
    return pl.pallas_call(
        _ffn_kernel,
        grid=(t // tm, d_ff // tf),
        in_specs=[pl.BlockSpec((tm, d), lambda i, f: (i, 0)),
                  pl.BlockSpec((d, tf), lambda i, f: (0, f), **mode),
                  pl.BlockSpec((d, tf), lambda i, f: (0, f), **mode),
                  pl.BlockSpec((tf, d), lambda i, f: (f, 0), **mode),
                  pl.BlockSpec((1, d), lambda i, f: (0, 0)),
                  pl.BlockSpec((1, d), lambda i, f: (0, 0))],
        out_specs=pl.BlockSpec((tm, d), lambda i, f: (i, 0)),
        out_shape=jax.ShapeDtypeStruct((t, d), F32),
        scratch_shapes=[pltpu.VMEM((tm, d), BF16), pltpu.VMEM((tm, d), F32)],
        compiler_params=_cparams("parallel", "arbitrary"),
        name="dense_ffn",
    )(x, w_gate.astype(BF16), w_up.astype(BF16), w_down.astype(BF16),
      ln_g.reshape(1, -1), ln_b.reshape(1, -1))


def _conv_mixer_kernel(seq, x_ref, halo_ref, win_ref, cw_ref, wout_ref, lng_ref, lnb_ref, o_ref):
    tm, d = x_ref.shape
    i = pl.program_id(0)
    x = x_ref[...]
    seq_start = (i * tm) % seq == 0
    xa = jnp.concatenate([halo_ref[...], x], axis=0).astype(BF16)
    z = _dot(xa, win_ref[...])
    gb = z[V7X_SUBLANES:, :d]
    u = z[:, d:2 * d] * z[:, 2 * d:]
    row = lax.broadcasted_iota(jnp.int32, u.shape, 0)
    u = jnp.where((row < V7X_SUBLANES) & seq_start, 0.0, u)
    cw = cw_ref[...]
    conv = pltpu.roll(u, 2, axis=0)[V7X_SUBLANES:] * cw[0:1]
    conv = conv + pltpu.roll(u, 1, axis=0)[V7X_SUBLANES:] * cw[1:2]
    conv = conv + u[V7X_SUBLANES:] * cw[2:3]
    y = _dot((gb * conv).astype(BF16), wout_ref[...])
    o_ref[...] = _layer_norm(ALPHA * x + y, lng_ref[...], lnb_ref[...])


def _conv_mixer(x, seq, w_in, conv_w, w_out, ln_g, ln_b):
    t, d = x.shape
    tm = TOKEN_TILE
    hb = tm // V7X_SUBLANES
    small = [w_in.astype(BF16), conv_w, w_out.astype(BF16), ln_g.reshape(1, -1), ln_b.reshape(1, -1)]
    return pl.pallas_call(
        functools.partial(_conv_mixer_kernel, seq),
        grid=(t // tm,),
        in_specs=[pl.BlockSpec((tm, d), lambda i: (i, 0)),
                  pl.BlockSpec((V7X_SUBLANES, d), lambda i: (jnp.maximum(i * hb - 1, 0), 0))]
                 + [pl.BlockSpec(a.shape, lambda i: (0, 0)) for a in small],
        out_specs=pl.BlockSpec((tm, d), lambda i: (i, 0)),
        out_shape=jax.ShapeDtypeStruct((t, d), F32),
        compiler_params=_cparams("parallel"),
        name="conv_mixer",
    )(x, x, *small)


def _router_kernel(x_ref, w_ref, idx_ref, prob_ref):
    logits = jnp.dot(x_ref[...], w_ref[...], preferred_element_type=F32,
                     precision=lax.Precision.HIGHEST)
    lane = lax.broadcasted_iota(jnp.int32, logits.shape, 1)
    logits = jnp.where(lane < N_EXPERTS, logits, -jnp.inf)
    m1 = jnp.max(logits, axis=-1, keepdims=True)
    i1 = jnp.min(jnp.where(logits == m1, lane, ROUTER_LANES), axis=-1, keepdims=True)
    rest = jnp.where(lane == i1, -jnp.inf, logits)
    m2 = jnp.max(rest, axis=-1, keepdims=True)
    i2 = jnp.min(jnp.where(rest == m2, lane, ROUTER_LANES), axis=-1, keepdims=True)
    e2 = jnp.exp(m2 - m1)
    den = 1.0 + e2
    idx_ref[...] = jnp.where(lane == 0, i1, jnp.where(lane == 1, i2, 0))
    prob_ref[...] = jnp.where(lane == 0, 1.0 / den, jnp.where(lane == 1, e2 / den, 0.0))


def _router(x, router_w):
    t, d = x.shape
    tm = TOKEN_TILE
    w = jnp.zeros((d, ROUTER_LANES), F32).at[:, :N_EXPERTS].set(router_w)
    out = pl.BlockSpec((tm, ROUTER_LANES), lambda i: (i, 0))
    idx, prob = pl.pallas_call(
        _router_kernel,
        grid=(t // tm,),
        in_specs=[pl.BlockSpec((tm, d), lambda i: (i, 0)),
                  pl.BlockSpec((d, ROUTER_LANES), lambda i: (0, 0))],
        out_specs=[out, out],
        out_shape=[jax.ShapeDtypeStruct((t, ROUTER_LANES), jnp.int32),
                   jax.ShapeDtypeStruct((t, ROUTER_LANES), F32)],
        compiler_params=_cparams("parallel"),
        name="router",
    )(x, w)
    return idx[:, :TOP_K], prob


def _to_slots(slot_ref, x):
    rows = x.shape[0]
    for s in range(V7X_SUBLANES):
        slot_ref[pl.ds(s, rows, stride=SLOT_ROWS), :] = x[:, s * V7X_LANES:(s + 1) * V7X_LANES]
    slot_ref[pl.ds(V7X_SUBLANES, rows, stride=SLOT_ROWS), :] = jnp.zeros((rows, V7X_LANES), x.dtype)


def _from_slots(slot_ref, rows):
    return jnp.concatenate([slot_ref[pl.ds(s, rows, stride=SLOT_ROWS), :] for s in range(V7X_SUBLANES)],
                           axis=-1)


def _dispatch_kernel(cnt_ref, start_ref, end_ref, pos_ref, h_ref, xs_ref, buf_ref, zero_ref, sem):
    tm = h_ref.shape[0]
    rows = xs_ref.shape[0] // SLOT_ROWS
    _to_slots(buf_ref, h_ref[...])

    def row_copy(r, k):
        return pltpu.make_async_copy(buf_ref.at[pl.ds(r * SLOT_ROWS, SLOT_ROWS)],
                                     xs_ref.at[pl.ds(pos_ref[0, k, r] * SLOT_ROWS, SLOT_ROWS)], sem)

    def zero_copy(r):
        return pltpu.make_async_copy(zero_ref, xs_ref.at[pl.ds(r * SLOT_ROWS, SLOT_ROWS)], sem)

    def for_each_padding_row(fn):
        for e in range(N_EXPERTS):
            lax.fori_loop(start_ref[e] + cnt_ref[e], end_ref[e], fn, 0)
        lax.fori_loop(end_ref[N_EXPERTS - 1], rows, fn, 0)

    def start(r, c):
        row_copy(r, 0).start(priority=0)
        row_copy(r, 1).start(priority=1)
        return c

    def wait(r, c):
        row_copy(r, 0).wait()
        row_copy(r, 1).wait()
        return c

    def zero_start(r, c):
        zero_copy(r).start()
        return c

    def zero_wait(r, c):
        zero_copy(r).wait()
        return c

    lax.fori_loop(0, tm, start, 0)

    @pl.when(pl.program_id(0) == 0)
    def _():
        zero_ref[...] = jnp.zeros_like(zero_ref)
        for_each_padding_row(zero_start)
        for_each_padding_row(zero_wait)

    lax.fori_loop(0, tm, wait, 0)


def _moe_dispatch(h, pos_tiles, counts, starts, ends, rows):
    t, d = h.shape
    assert d == V7X_SUBLANES * V7X_LANES, "one token must fill one (8, 128) tile"
    tm = pos_tiles.shape[2]
    return pl.pallas_call(
        _dispatch_kernel,
        grid_spec=pltpu.PrefetchScalarGridSpec(
            num_scalar_prefetch=3,
            grid=(t // tm,),
            in_specs=[pl.BlockSpec((1, TOP_K, tm), lambda i, *_: (i, 0, 0), memory_space=pltpu.SMEM),
                      pl.BlockSpec((tm, d), lambda i, *_: (i, 0))],
            out_specs=pl.BlockSpec(memory_space=pl.ANY),
            scratch_shapes=[pltpu.VMEM((tm * SLOT_ROWS, V7X_LANES), F32),
                            pltpu.VMEM((SLOT_ROWS, V7X_LANES), F32),
                            pltpu.SemaphoreType.DMA(())]),
        out_shape=jax.ShapeDtypeStruct((rows * SLOT_ROWS, V7X_LANES), F32),
        compiler_params=_cparams("arbitrary"),
        name="moe_dispatch",
    )(counts, starts, ends, pos_tiles, h)


def _moe_kernel(te_ref, nv_ref, x_ref, wg_ref, wu_ref, wd_ref, o_ref, xb_ref, acc_ref):
    i = pl.program_id(0)
    f = pl.program_id(1)
    last = pl.num_programs(1) - 1
    valid = i < nv_ref[0]
    tg = acc_ref.shape[0]

    @pl.when(valid & (f == 0))
    def _():
        xb_ref[...] = _from_slots(x_ref, tg).astype(BF16)
        acc_ref[...] = jnp.zeros_like(acc_ref)

    @pl.when(valid)
    def _():
        x = xb_ref[...]
        hid = jax.nn.silu(_dot(x, wg_ref[0])) * _dot(x, wu_ref[0])
        acc_ref[...] += _dot(hid.astype(BF16), wd_ref[0])

    @pl.when(valid & (f == last))
    def _():
        _to_slots(o_ref, acc_ref[...])

    @pl.when(jnp.logical_not(valid) & (f == last))
    def _():
        o_ref[...] = jnp.zeros_like(o_ref)


def _moe_experts(xs, tile_expert, n_valid, w_gate, w_up, w_down):
    d = V7X_SUBLANES * V7X_LANES
    r = xs.shape[0] // SLOT_ROWS
    d_ff = w_gate.shape[2]
    tg = MOE_ROW_TILE
    tf = _ffn_chunk(d_ff, MOE_FF_CHUNK)
    nf = d_ff // tf

    def f_eff(i, f, nv):
        return jnp.where(i < nv[0], f, nf - 1)

    row_tile = pl.BlockSpec((tg * SLOT_ROWS, V7X_LANES), lambda i, f, te, nv: (i, 0))
    return pl.pallas_call(
        _moe_kernel,
        grid_spec=pltpu.PrefetchScalarGridSpec(
            num_scalar_prefetch=2,
            grid=(r // tg, nf),
            in_specs=[row_tile,
                      pl.BlockSpec((1, d, tf), lambda i, f, te, nv: (te[i], 0, f_eff(i, f, nv))),
                      pl.BlockSpec((1, d, tf), lambda i, f, te, nv: (te[i], 0, f_eff(i, f, nv))),
                      pl.BlockSpec((1, tf, d), lambda i, f, te, nv: (te[i], f_eff(i, f, nv), 0))],
            out_specs=row_tile,
            scratch_shapes=[pltpu.VMEM((tg, d), BF16), pltpu.VMEM((tg, d), F32)]),
        out_shape=jax.ShapeDtypeStruct(xs.shape, F32),
        compiler_params=_cparams("parallel", "arbitrary"),
        name="moe_experts",
    )(tile_expert, n_valid, xs, w_gate, w_up, w_down)


def _combine_kernel(pos_ref, h_ref, p_ref, y_ref, lng_ref, lnb_ref, o_ref, buf_ref, sem):
    tm = h_ref.shape[0]

    def row_copy(r, k):
        return pltpu.make_async_copy(y_ref.at[pl.ds(pos_ref[0, k, r] * SLOT_ROWS, SLOT_ROWS)],
                                     buf_ref.at[k, pl.ds(r * SLOT_ROWS, SLOT_ROWS)], sem)

    def start(r, c):
        row_copy(r, 0).start(priority=0)
        row_copy(r, 1).start(priority=1)
        return c

    def wait(r, c):
        row_copy(r, 0).wait()
        row_copy(r, 1).wait()
        return c

    lax.fori_loop(0, tm, start, 0)
    lax.fori_loop(0, tm, wait, 0)
    p = p_ref[...]
    ffn = p[:, 0:1] * _from_slots(buf_ref.at[0], tm) + p[:, 1:2] * _from_slots(buf_ref.at[1], tm)
    o_ref[...] = _layer_norm(ALPHA * h_ref[...] + ffn, lng_ref[...], lnb_ref[...])


def _moe_combine(h, top_p, y, pos_tiles, ln_g, ln_b):
    t, d = h.shape
    tm = pos_tiles.shape[2]
    return pl.pallas_call(
        _combine_kernel,
        grid=(t // tm,),
        in_specs=[pl.BlockSpec((1, TOP_K, tm), lambda i: (i, 0, 0), memory_space=pltpu.SMEM),
                  pl.BlockSpec((tm, d), lambda i: (i, 0)),
                  pl.BlockSpec((tm, ROUTER_LANES), lambda i: (i, 0)),
                  pl.BlockSpec(memory_space=pl.ANY),
                  pl.BlockSpec((1, d), lambda i: (0, 0)),
                  pl.BlockSpec((1, d), lambda i: (0, 0))],
        out_specs=pl.BlockSpec((tm, d), lambda i: (i, 0)),
        out_shape=jax.ShapeDtypeStruct((t, d), F32),
        scratch_shapes=[pltpu.VMEM((TOP_K, tm * SLOT_ROWS, V7X_LANES), F32),
                        pltpu.SemaphoreType.DMA(())],
        compiler_params=_cparams("arbitrary"),
        name="moe_combine",
    )(pos_tiles, h, top_p, y, ln_g.reshape(1, -1), ln_b.reshape(1, -1))


def _route_plan(top_idx, t):
    tg = MOE_ROW_TILE
    e_flat = top_idx.T.reshape(-1)
    onehot = (e_flat[:, None] == jnp.arange(N_EXPERTS)[None, :]).astype(jnp.int32)
    csum = jnp.cumsum(onehot, axis=0)
    counts = csum[-1]
    padded = (counts + tg - 1) // tg * tg
    ends = jnp.cumsum(padded)
    starts = ends - padded
    pos = jnp.sum(onehot * (csum - 1 + starts[None, :]), axis=1)
    rows = TOP_K * t + N_EXPERTS * tg
    tile_start = jnp.arange(rows // tg, dtype=jnp.int32) * tg
    tile_expert = jnp.sum((ends[None, :] <= tile_start[:, None]).astype(jnp.int32), axis=1)
    tile_expert = jnp.minimum(tile_expert, N_EXPERTS - 1)
    n_valid = (ends[N_EXPERTS - 1] // tg).reshape(1)
    tm = GATHER_TILE
    pos_tiles = jnp.transpose(pos.reshape(TOP_K, t // tm, tm), (1, 0, 2))
    return pos_tiles, counts, starts, ends, tile_expert, n_valid, rows


def _moe(h, router_w, w_gate, w_up, w_down, ln_g, ln_b):
    top_idx, top_p = _router(h, router_w)
    pos_tiles, counts, starts, ends, tile_expert, n_valid, rows = _route_plan(top_idx, h.shape[0])
    xs = _moe_dispatch(h, pos_tiles, counts, starts, ends, rows)
    y = _moe_experts(xs, tile_expert, n_valid,
                     w_gate.astype(BF16), w_up.astype(BF16), w_down.astype(BF16))
    return _moe_combine(h, top_p, y, pos_tiles, ln_g, ln_b)


_DIM_MAJOR = (np.arange(B_WIDTH) % B_HEADS) * HEAD_DIM + np.arange(B_WIDTH) // B_HEADS


def _dim_major_params(w_in, mu, w0, w_up, a0, a_up, g_up, k_k, k_a, r_k, gn_g, gn_b, w_out):
    pm = _DIM_MAJOR
    cols = np.concatenate([pm, B_WIDTH + pm, 2 * B_WIDTH + pm, np.arange(3 * B_WIDTH, B_IN_WIDTH)])
    w_in = jnp.concatenate([w_in[:, :QKV_WIDTH], w_in[:, QKV_WIDTH + cols]], axis=1)
    w_out = jnp.concatenate([w_out[:A_WIDTH], w_out[A_WIDTH + pm]], axis=0)
    return (w_in, mu[cols], w0[pm], w_up[:, pm], a0[pm], a_up[:, pm], g_up[:, pm], k_k[pm], k_a[pm],
            r_k.reshape(-1)[pm], gn_g[pm], gn_b[pm], w_out)


def _v_to_scan_layout(v, bsz, seq):
    half = HEAD_DIM // 2
    v = v.reshape(bsz, seq, 2, half, B_HEADS)
    return jnp.transpose(v, (1, 3, 2, 0, 4)).reshape(seq, half, V7X_LANES)


def _y_from_scan_layout(y, bsz, seq):
    half = HEAD_DIM // 2
    y = y.reshape(seq, half, 2, bsz, B_HEADS)
    return jnp.transpose(y, (3, 0, 2, 1, 4)).reshape(bsz * seq, B_WIDTH)


def kernel(x, rel_bias_table, even_w_in, even_sinks, rwkv_mu, rwkv_w0, rwkv_w_up, rwkv_a0, rwkv_a_up, rwkv_g_up, rwkv_k_k, rwkv_k_a, rwkv_r_k, rwkv_gn_g, rwkv_gn_b, even_w_out, even_ln_mix_g, even_ln_mix_b, dense_w_gate, dense_w_up, dense_w_down, even_ln_ffn_g, even_ln_ffn_b, odd_w_in, odd_conv_w, odd_w_out, odd_ln_mix_g, odd_ln_mix_b, router_w, moe_w_gate, moe_w_up, moe_w_down, odd_ln_ffn_g, odd_ln_ffn_b):
    bsz, seq, d = x.shape
    assert seq % TOKEN_TILE == 0 and seq % BLOCK == 0 and seq % SCAN_CHUNK == 0
    assert bsz * B_HEADS * 2 == V7X_LANES, "scan layout puts (half, batch, head) on the lanes"
    h = x.reshape(bsz * seq, d)
    ones = jnp.asarray(np.kron(np.ones((HEAD_DIM, HEAD_DIM)), np.eye(B_HEADS)), BF16)
    bias = _rel_bias(rel_bias_table)
    for layer in range(DEPTH):
        i = layer // 2
        if layer % 2 == 0:
            (w_in, mu, w0, w_up, a0, a_up, g_up, k_k, k_a, r_k, gn_g, gn_b, w_out) = _dim_major_params(
                even_w_in[i], rwkv_mu[i], rwkv_w0[i], rwkv_w_up[i], rwkv_a0[i], rwkv_a_up[i],
                rwkv_g_up[i], rwkv_k_k[i], rwkv_k_a[i], rwkv_r_k[i], rwkv_gn_g[i], rwkv_gn_b[i],
                even_w_out[i])
            qkv, zb = _in_proj(h, w_in.astype(BF16))
            ya = _attention(qkv, even_sinks[i], bias, bsz, seq)
            p, v, bonus, g = _rwkv_prep(zb, seq, mu, w0, w_up, a0, a_up, g_up, k_k, k_a, r_k, ones)
            y = _rwkv_scan(p, _v_to_scan_layout(v, bsz, seq), bsz)
            y = _y_from_scan_layout(y, bsz, seq)
            h = _mix_out(ya, y, bonus, g, h, gn_g, gn_b, ones, w_out,
                         even_ln_mix_g[i], even_ln_mix_b[i])
            h = _dense_ffn(h, dense_w_gate[i], dense_w_up[i], dense_w_down[i],
                           even_ln_ffn_g[i], even_ln_ffn_b[i])
        else:
            h = _conv_mixer(h, seq, odd_w_in[i], odd_conv_w[i], odd_w_out[i],
                            odd_ln_mix_g[i], odd_ln_mix_b[i])
            h = _moe(h, router_w[i], moe_w_gate[i], moe_w_up[i], moe_w_down[i],
                     odd_ln_ffn_g[i], odd_ln_ffn_b[i])
    return h.reshape(bsz, seq, d)
```

```python
import functools
import math

import numpy as np
import jax
import jax.numpy as jnp
from jax import lax
from jax.experimental import pallas as pl
from jax.experimental.pallas import tpu as pltpu

F32 = jnp.float32
BF16 = jnp.bfloat16

HEAD_DIM = 64
A_Q_HEADS = 8
A_KV_HEADS = 2
A_GROUP = A_Q_HEADS // A_KV_HEADS
WINDOW = 128
BLOCK = 128
NUM_BUCKETS = 32
MAX_DISTANCE = 128
B_HEADS = 8
A_WIDTH = A_Q_HEADS * HEAD_DIM
A_KV_WIDTH = A_KV_HEADS * HEAD_DIM
QKV_WIDTH = A_WIDTH + 2 * A_KV_WIDTH
B_WIDTH = B_HEADS * HEAD_DIM
DECAY_LORA = 64
AAA_LORA = 64
GATE_LORA = 128
B_IN_WIDTH = 3 * B_WIDTH + DECAY_LORA + AAA_LORA + GATE_LORA
CONV_WIDTH = 3
N_EXPERTS = 8
TOP_K = 2
DEPTH = 2
ALPHA = (2.0 * DEPTH) ** 0.25
LN_EPS = 1e-5
GN_EPS = 64e-5
NEG_INF = -1e30
ATTN_SCALE = HEAD_DIM ** -0.5

V7X_LANES = 128
V7X_SUBLANES = 8
V7X_MXU_DIM = 256
V7X_VMEM_LIMIT = 56 * 1024 * 1024
DENSE_FF_CHUNK = 4096
MOE_FF_CHUNK = 1792

TOKEN_TILE = 512
SCAN_CHUNK = 128
MOE_ROW_TILE = 512
GATHER_TILE = 256
ROUTER_LANES = 128
SLOT_ROWS = 9


def _cparams(*sem):
    return pltpu.CompilerParams(dimension_semantics=sem, vmem_limit_bytes=V7X_VMEM_LIMIT)


def _layer_norm(y, g, b):
    mu = jnp.mean(y, axis=-1, keepdims=True)
    d = y - mu
    var = jnp.mean(d * d, axis=-1, keepdims=True)
    return d * lax.rsqrt(var + LN_EPS) * g + b


def _dot(a, b):
    return jnp.dot(a, b, preferred_element_type=F32)


def _head_sum(x, ones_blockdiag):
    hi = x.astype(BF16)
    lo = (x - hi.astype(F32)).astype(BF16)
    return _dot(hi, ones_blockdiag) + _dot(lo, ones_blockdiag)


def _in_proj_kernel(x_ref, w_ref, qkv_ref, zb_ref):
    z = _dot(x_ref[...].astype(BF16), w_ref[...])
    qkv_ref[...] = z[:, :QKV_WIDTH].astype(BF16)
    zb_ref[...] = z[:, QKV_WIDTH:]


def _in_proj(h, w):
    t, d = h.shape
    n = w.shape[1]
    tm = TOKEN_TILE
    return pl.pallas_call(
        _in_proj_kernel,
        grid=(t // tm,),
        in_specs=[pl.BlockSpec((tm, d), lambda i: (i, 0)),
                  pl.BlockSpec((d, n), lambda i: (0, 0))],
        out_specs=[pl.BlockSpec((tm, QKV_WIDTH), lambda i: (i, 0)),
                   pl.BlockSpec((tm, n - QKV_WIDTH), lambda i: (i, 0))],
        out_shape=[jax.ShapeDtypeStruct((t, QKV_WIDTH), BF16),
                   jax.ShapeDtypeStruct((t, n - QKV_WIDTH), F32)],
        compiler_params=_cparams("parallel"),
        name="in_proj",
    )(h, w)


def _bucket_table():
    qi = np.arange(BLOCK)[:, None]
    ki = np.arange(2 * BLOCK)[None, :]
    n = np.maximum(qi + BLOCK - ki, 0)
    max_exact = NUM_BUCKETS // 2
    log_ratio = (np.log(np.maximum(n, 1).astype(np.float32) / max_exact)
                 / math.log(MAX_DISTANCE / max_exact))
    large = max_exact + (log_ratio * (NUM_BUCKETS - max_exact)).astype(np.int32)
    large = np.minimum(large, NUM_BUCKETS - 1)
    return np.where(n < max_exact, n, large).astype(np.int32)


def _bias_kernel(tab_ref, bucket_ref, out_ref):
    bucket = bucket_ref[...]
    qi = lax.broadcasted_iota(jnp.int32, bucket.shape, 0)
    ki = lax.broadcasted_iota(jnp.int32, bucket.shape, 1)
    dist = qi + BLOCK - ki
    in_window = (dist >= 0) & (dist < WINDOW)
    for h in range(A_Q_HEADS):
        acc = jnp.zeros(bucket.shape, F32)
        for b in range(NUM_BUCKETS):
            acc = jnp.where(bucket == b, tab_ref[b, h], acc)
        out_ref[0, h] = jnp.where(in_window & (ki >= BLOCK), acc, NEG_INF)
        out_ref[1, h] = jnp.where(in_window, acc, NEG_INF)


def _rel_bias(table):
    return pl.pallas_call(
        _bias_kernel,
        in_specs=[pl.BlockSpec(memory_space=pltpu.SMEM),
                  pl.BlockSpec(memory_space=pltpu.VMEM)],
        out_specs=pl.BlockSpec(memory_space=pltpu.VMEM),
        out_shape=jax.ShapeDtypeStruct((2, A_Q_HEADS, BLOCK, 2 * BLOCK), F32),
        name="rel_bias",
    )(table, jnp.asarray(_bucket_table()))


def _attn_kernel(sink_ref, q_ref, kp_ref, kc_ref, vp_ref, vc_ref, bias_ref, o_ref):
    q = q_ref[...] * ATTN_SCALE
    kcat = jnp.concatenate([kp_ref[...], kc_ref[...]], axis=0)
    vcat = jnp.concatenate([vp_ref[...], vc_ref[...]], axis=0)
    outs = []
    for hk in range(A_KV_HEADS):
        k_h = kcat[:, hk * HEAD_DIM:(hk + 1) * HEAD_DIM]
        v_h = vcat[:, hk * HEAD_DIM:(hk + 1) * HEAD_DIM]
        for g in range(A_GROUP):
            hq = hk * A_GROUP + g
            q_h = q[:, hq * HEAD_DIM:(hq + 1) * HEAD_DIM]
            s = lax.dot_general(q_h, k_h, (((1,), (1,)), ((), ())),
                                preferred_element_type=F32) + bias_ref[0, hq]
            sink = sink_ref[hq]
            m = jnp.maximum(jnp.max(s, axis=-1, keepdims=True), sink)
            p = jnp.exp(s - m)
            denom = jnp.sum(p, axis=-1, keepdims=True) + jnp.exp(sink - m)
            o = _dot(p.astype(BF16), v_h)
            outs.append(o / denom)
    o_ref[...] = jnp.concatenate(outs, axis=-1).astype(o_ref.dtype)


def _attention(qkv, sinks, bias, bsz, seq):
    t = qkv.shape[0]
    nb = seq // BLOCK
    kcol = A_WIDTH // A_KV_WIDTH
    vcol = kcol + 1

    def cur(b, n):
        return b * nb + n

    def prev(b, n):
        return b * nb + jnp.maximum(n - 1, 0)

    return pl.pallas_call(
        _attn_kernel,
        grid=(bsz, nb),
        in_specs=[pl.BlockSpec(memory_space=pltpu.SMEM),
                  pl.BlockSpec((BLOCK, A_WIDTH), lambda b, n: (cur(b, n), 0)),
                  pl.BlockSpec((BLOCK, A_KV_WIDTH), lambda b, n: (prev(b, n), kcol)),
                  pl.BlockSpec((BLOCK, A_KV_WIDTH), lambda b, n: (cur(b, n), kcol)),
                  pl.BlockSpec((BLOCK, A_KV_WIDTH), lambda b, n: (prev(b, n), vcol)),
                  pl.BlockSpec((BLOCK, A_KV_WIDTH), lambda b, n: (cur(b, n), vcol)),
                  pl.BlockSpec((1, A_Q_HEADS, BLOCK, 2 * BLOCK),
                               lambda b, n: (jnp.minimum(n, 1), 0, 0, 0))],
        out_specs=pl.BlockSpec((BLOCK, A_WIDTH), lambda b, n: (cur(b, n), 0)),
        out_shape=jax.ShapeDtypeStruct((t, A_WIDTH), BF16),
        compiler_params=_cparams("parallel", "parallel"),
        name="swa_attention",
    )(sinks, qkv, qkv, qkv, qkv, qkv, bias)


def _rwkv_prep_kernel(seq, zb_ref, halo_ref, mu_ref, w0_ref, wup_ref, a0_ref, aup_ref,
                      gup_ref, kk_ref, ka_ref, rk_ref, ones_ref,
                      p_ref, v_ref, bonus_ref, g_ref):
    tm = zb_ref.shape[0]
    i = pl.program_id(0)
    z = zb_ref[...]
    seq_start = (i * tm) % seq == 0
    prev_last = jnp.where(seq_start, 0.0, halo_ref[V7X_SUBLANES - 1:V7X_SUBLANES, :])
    row = lax.broadcasted_iota(jnp.int32, z.shape, 0)
    zsh = jnp.where(row == 0, prev_last, pltpu.roll(z, 1, axis=0))
    zs = z + (zsh - z) * mu_ref[...]
    c = B_WIDTH
    zr, zk, zv = zs[:, :c], zs[:, c:2 * c], zs[:, 2 * c:3 * c]
    zwd = zs[:, 3 * c:3 * c + DECAY_LORA]
    zad = zs[:, 3 * c + DECAY_LORA:3 * c + DECAY_LORA + AAA_LORA]
    zgd = zs[:, 3 * c + DECAY_LORA + AAA_LORA:]
    ones = ones_ref[...]

    w = -jax.nn.softplus(-(w0_ref[...] + _dot(jnp.tanh(zwd).astype(BF16), wup_ref[...]))) - 0.5
    decay = jnp.exp(-jnp.exp(w))
    a = jax.nn.sigmoid(a0_ref[...] + _dot(zad.astype(BF16), aup_ref[...]))
    g = _dot(jax.nn.sigmoid(zgd).astype(BF16), gup_ref[...])

    kk = zk * kk_ref[...]
    kk = kk / jnp.maximum(jnp.sqrt(_head_sum(kk * kk, ones)), 1e-12)
    k = zk * (1.0 + (a - 1.0) * ka_ref[...])
    p_ref[0] = zr.T
    p_ref[1] = decay.T
    p_ref[2] = k.T
    p_ref[3] = (-kk).T
    p_ref[4] = (kk * a).T
    v_ref[...] = zv
    bonus_ref[...] = _head_sum(zr * k * rk_ref[...], ones) * zv
    g_ref[...] = g


def _rwkv_prep(zb, seq, mu, w0, w_up, a0, a_up, g_up, k_k, k_a, r_k, ones):
    t, n = zb.shape
    tm = TOKEN_TILE
    hb = tm // V7X_SUBLANES
    c = B_WIDTH

    def row(v):
        return v.reshape(1, -1)

    def full(a):
        return pl.BlockSpec(a.shape, lambda i: (0,) * a.ndim)

    small = [row(mu), row(w0), w_up.astype(BF16), row(a0), a_up.astype(BF16),
             g_up.astype(BF16), row(k_k), row(k_a), row(r_k), ones]
    tok = pl.BlockSpec((tm, c), lambda i: (i, 0))
    return pl.pallas_call(
        functools.partial(_rwkv_prep_kernel, seq),
        grid=(t // tm,),
        in_specs=[pl.BlockSpec((tm, n), lambda i: (i, 0)),
                  pl.BlockSpec((V7X_SUBLANES, n), lambda i: (jnp.maximum(i * hb - 1, 0), 0))]
                 + [full(a) for a in small],
        out_specs=[pl.BlockSpec((5, c, tm), lambda i: (0, 0, i)), tok, tok, tok],
        out_shape=[jax.ShapeDtypeStruct((5, c, t), F32)]
                  + [jax.ShapeDtypeStruct((t, c), F32)] * 3,
        compiler_params=_cparams("parallel"),
        name="rwkv_prep",
    )(zb, zb, *small)


def _rwkv_scan_kernel(p_hbm, v_ref, y_ref, s_ref, pin_ref, p2_ref, sem):
    c = pl.program_id(0)
    nc = pl.num_programs(0)
    n_b, n_vec = pin_ref.shape[:2]
    n_steps = v_ref.shape[0]

    def chunk_copy(chunk, b):
        src = p_hbm.at[:, :, pl.ds(pl.multiple_of((b * nc + chunk) * n_steps, n_steps), n_steps)]
        return pltpu.make_async_copy(src, pin_ref.at[b], sem.at[b])

    @pl.when(c == 0)
    def _():
        s_ref[...] = jnp.zeros_like(s_ref)
        for b in range(n_b):
            chunk_copy(0, b).start()

    for b in range(n_b):
        chunk_copy(c, b).wait()

    def relayout(j, carry):
        r0 = pl.multiple_of(j * B_HEADS, B_HEADS)
        for which in range(n_vec):
            x = jnp.concatenate([pin_ref[b, which, pl.ds(r0, B_HEADS), :] for b in range(n_b)] * 2,
                                axis=0)
            p2_ref[j * n_vec + which, pl.ds(0, n_steps), :] = x.T
        return carry

    lax.fori_loop(0, HEAD_DIM, relayout, 0)

    @pl.when(c + 1 < nc)
    def _():
        for b in range(n_b):
            chunk_copy(c + 1, b).start()

    def vec(t, which, j):
        return p2_ref[j * n_vec + which, pl.ds(t, 1), :]

    zero = jnp.zeros(v_ref.shape[1:], F32)

    sa_first = [zero, zero]
    for j in range(HEAD_DIM):
        sa_first[j % 2] = sa_first[j % 2] + s_ref[j] * vec(0, 3, j)

    def step(t, sa):
        v_t = v_ref[t]
        t_next = jnp.minimum(t + 1, n_steps - 1)
        y = [zero, zero]
        sa_next = [zero, zero]
        for j in range(HEAD_DIM):
            s = s_ref[j] * vec(t, 1, j) + sa * vec(t, 4, j) + v_t * vec(t, 2, j)
            s_ref[j] = s
            y[j % 2] = y[j % 2] + s * vec(t, 0, j)
            sa_next[j % 2] = sa_next[j % 2] + s * vec(t_next, 3, j)
        y_ref[t] = y[0] + y[1]
        return sa_next[0] + sa_next[1]

    lax.fori_loop(0, n_steps, step, sa_first[0] + sa_first[1])


def _rwkv_scan(p, v, bsz):
    n_vec, c, t = p.shape
    seq = t // bsz
    tt = SCAN_CHUNK
    nc = seq // tt
    half = HEAD_DIM // 2
    slab_rows = tt + V7X_SUBLANES
    return pl.pallas_call(
        _rwkv_scan_kernel,
        grid=(nc,),
        in_specs=[pl.BlockSpec(memory_space=pl.ANY),
                  pl.BlockSpec((tt, half, V7X_LANES), lambda i: (i, 0, 0))],
        out_specs=pl.BlockSpec((tt, half, V7X_LANES), lambda i: (i, 0, 0)),
        out_shape=jax.ShapeDtypeStruct((seq, half, V7X_LANES), F32),
        scratch_shapes=[pltpu.VMEM((HEAD_DIM, half, V7X_LANES), F32),
                        pltpu.VMEM((bsz, n_vec, c, tt), F32),
                        pltpu.VMEM((HEAD_DIM * n_vec, slab_rows, V7X_LANES), F32),
                        pltpu.SemaphoreType.DMA((bsz,))],
        compiler_params=_cparams("arbitrary"),
        name="rwkv_scan",
    )(p, v)


def _mix_out_kernel(ya_ref, y_ref, bonus_ref, g_ref, h_ref, gng_ref, gnb_ref, ones_ref,
                    wa_ref, wb_ref, lng_ref, lnb_ref, o_ref):
    ones = ones_ref[...]
    y = y_ref[...]
    inv = 1.0 / HEAD_DIM
    d = y - _head_sum(y, ones) * inv
    var = _head_sum(d * d, ones) * inv
    yn = d * lax.rsqrt(var + GN_EPS) * gng_ref[...] + gnb_ref[...]
    yb = (yn + bonus_ref[...]) * g_ref[...]
    mix = _dot(ya_ref[...], wa_ref[...]) + _dot(yb.astype(BF16), wb_ref[...])
    o_ref[...] = _layer_norm(ALPHA * h_ref[...] + mix, lng_ref[...], lnb_ref[...])


def _mix_out(ya, y, bonus, g, h, gn_g, gn_b, ones, w_out, ln_g, ln_b):
    t, d = h.shape
    tm = TOKEN_TILE
    c = B_WIDTH
    wa = w_out[:A_WIDTH].astype(BF16)
    wb = w_out[A_WIDTH:].astype(BF16)
    small = [gn_g.reshape(1, -1), gn_b.reshape(1, -1), ones, wa, wb,
             ln_g.reshape(1, -1), ln_b.reshape(1, -1)]
    tok = pl.BlockSpec((tm, c), lambda i: (i, 0))
    return pl.pallas_call(
        _mix_out_kernel,
        grid=(t // tm,),
        in_specs=[tok, tok, tok, tok, pl.BlockSpec((tm, d), lambda i: (i, 0))]
                 + [pl.BlockSpec(a.shape, lambda i: (0, 0)) for a in small],
        out_specs=pl.BlockSpec((tm, d), lambda i: (i, 0)),
        out_shape=jax.ShapeDtypeStruct((t, d), F32),
        compiler_params=_cparams("parallel"),
        name="mix_out",
    )(ya, y, bonus, g, h, *small)


def _ffn_kernel(x_ref, wg_ref, wu_ref, wd_ref, lng_ref, lnb_ref, o_ref, xb_ref, acc_ref):
    f = pl.program_id(1)

    @pl.when(f == 0)
    def _():
        xb_ref[...] = x_ref[...].astype(BF16)
        acc_ref[...] = jnp.zeros_like(acc_ref)

    xb = xb_ref[...]
    hid = jax.nn.silu(_dot(xb, wg_ref[...])) * _dot(xb, wu_ref[...])
    acc_ref[...] += _dot(hid.astype(BF16), wd_ref[...])

    @pl.when(f == pl.num_programs(1) - 1)
    def _():
        o_ref[...] = _layer_norm(ALPHA * x_ref[...] + acc_ref[...], lng_ref[...], lnb_ref[...])


def _ffn_chunk(d_ff, limit):
    for unit in (V7X_MXU_DIM, V7X_LANES):
        for n in range(limit // unit, 0, -1):
            if d_ff % (n * unit) == 0:
                return n * unit
    raise ValueError(f"d_ff={d_ff} is not a multiple of {V7X_LANES}")


def _dense_ffn(x, w_gate, w_up, w_down, ln_g, ln_b):
    t, d = x.shape
    d_ff = w_gate.shape[1]
    tm = TOKEN_TILE
    tf = _ffn_chunk(d_ff, DENSE_FF_CHUNK)
    mode = dict(pipeline_mode=pl.Buffered(1)) if tf == d_ff else {}
    return pl.pallas_call(
        _ffn_kernel,
        grid=(t // tm, d_ff // tf),
        in_specs=[pl.BlockSpec((tm, d), lambda i, f: (i, 0)),
                  pl.BlockSpec((d, tf), lambda i, f: (0, f), **mode),
                  pl.BlockSpec((d, tf), lambda i, f: (0, f), **mode),
                  pl.BlockSpec((tf, d), lambda i, f: (f, 0), **mode),
                  pl.BlockSpec((1, d), lambda i, f: (0, 0)),
                  pl.BlockSpec((1, d), lambda i, f: (0, 0))],
        out_specs=pl.BlockSpec((tm, d), lambda i, f: (i, 0)),
        out_shape=jax.ShapeDtypeStruct((t, d), F32),
        scratch_shapes=[pltpu.VMEM((tm, d), BF16), pltpu.VMEM((tm, d), F32)],
        compiler_params=_cparams("parallel", "arbitrary"),
        name="dense_ffn",
    )(x, w_gate.astype(BF16), w_up.astype(BF16), w_down.astype(BF16),
      ln_g.reshape(1, -1), ln_b.reshape(1, -1))


def _conv_mixer_kernel(seq, x_ref, halo_ref, win_ref, cw_ref, wout_ref, lng_ref, lnb_ref, rwt_ref,
                       o_ref, idx_ref, prob_ref):
    tm, d = x_ref.shape
    i = pl.program_id(0)
    x = x_ref[...]
    seq_start = (i * tm) % seq == 0
    xa = jnp.concatenate([halo_ref[...], x], axis=0).astype(BF16)
    z = _dot(xa, win_ref[...])
    gb = z[V7X_SUBLANES:, :d]
    u = z[:, d:2 * d] * z[:, 2 * d:]
    row = lax.broadcasted_iota(jnp.int32, u.shape, 0)
    u = jnp.where((row < V7X_SUBLANES) & seq_start, 0.0, u)
    cw = cw_ref[...]
    conv = pltpu.roll(u, 2, axis=0)[V7X_SUBLANES:] * cw[0:1]
    conv = conv + pltpu.roll(u, 1, axis=0)[V7X_SUBLANES:] * cw[1:2]
    conv = conv + u[V7X_SUBLANES:] * cw[2:3]
    y = _dot((gb * conv).astype(BF16), wout_ref[...])
    h = _layer_norm(ALPHA * x + y, lng_ref[...], lnb_ref[...])
    o_ref[...] = h
    _route_top2(h, rwt_ref, idx_ref, prob_ref)


def _conv_mixer(x, seq, w_in, conv_w, w_out, ln_g, ln_b, router_w):
    t, d = x.shape
    tm = TOKEN_TILE
    hb = tm // V7X_SUBLANES
    small = [w_in.astype(BF16), conv_w, w_out.astype(BF16), ln_g.reshape(1, -1), ln_b.reshape(1, -1),
             router_w.T]
    route = pl.BlockSpec((tm, ROUTER_LANES), lambda i: (i, 0))
    h, idx, prob = pl.pallas_call(
        functools.partial(_conv_mixer_kernel, seq),
        grid=(t // tm,),
        in_specs=[pl.BlockSpec((tm, d), lambda i: (i, 0)),
                  pl.BlockSpec((V7X_SUBLANES, d), lambda i: (jnp.maximum(i * hb - 1, 0), 0))]
                 + [pl.BlockSpec(a.shape, lambda i: (0, 0)) for a in small],
        out_specs=[pl.BlockSpec((tm, d), lambda i: (i, 0)), route, route],
        out_shape=[jax.ShapeDtypeStruct((t, d), F32),
                   jax.ShapeDtypeStruct((t, ROUTER_LANES), jnp.int32),
                   jax.ShapeDtypeStruct((t, ROUTER_LANES), F32)],
        compiler_params=_cparams("parallel"),
        name="conv_mixer",
    )(x, x, *small)
    return h, idx[:, :TOP_K], prob


def _route_top2(x, wt_ref, idx_ref, prob_ref):
    lane = lax.broadcasted_iota(jnp.int32, (x.shape[0], ROUTER_LANES), 1)
    logits = jnp.full((x.shape[0], ROUTER_LANES), -jnp.inf, F32)
    for e in range(N_EXPERTS):
        logit_e = jnp.sum(x * wt_ref[e:e + 1, :], axis=-1, keepdims=True)
        logits = jnp.where(lane == e, logit_e, logits)
    m1 = jnp.max(logits, axis=-1, keepdims=True)
    i1 = jnp.min(jnp.where(logits == m1, lane, ROUTER_LANES), axis=-1, keepdims=True)
    rest = jnp.where(lane == i1, -jnp.inf, logits)
    m2 = jnp.max(rest, axis=-1, keepdims=True)
    i2 = jnp.min(jnp.where(rest == m2, lane, ROUTER_LANES), axis=-1, keepdims=True)
    e2 = jnp.exp(m2 - m1)
    den = 1.0 + e2
    idx_ref[...] = jnp.where(lane == 0, i1, jnp.where(lane == 1, i2, 0))
    prob_ref[...] = jnp.where(lane == 0, 1.0 / den, jnp.where(lane == 1, e2 / den, 0.0))


def _to_slots(slot_ref, x):
    rows = x.shape[0]
    for s in range(V7X_SUBLANES):
        slot_ref[pl.ds(s, rows, stride=SLOT_ROWS), :] = x[:, s * V7X_LANES:(s + 1) * V7X_LANES]
    slot_ref[pl.ds(V7X_SUBLANES, rows, stride=SLOT_ROWS), :] = jnp.zeros((rows, V7X_LANES), x.dtype)


def _from_slots(slot_ref, rows):
    return jnp.concatenate([slot_ref[pl.ds(s, rows, stride=SLOT_ROWS), :] for s in range(V7X_SUBLANES)],
                           axis=-1)


def _dispatch_kernel(cnt_ref, start_ref, end_ref, pos_ref, h_ref, xs_ref, buf_ref, zero_ref, sem):
    tm = h_ref.shape[0]
    rows = xs_ref.shape[0] // SLOT_ROWS
    _to_slots(buf_ref, h_ref[...])

    def row_copy(r, k):
        return pltpu.make_async_copy(buf_ref.at[pl.ds(r * SLOT_ROWS, SLOT_ROWS)],
                                     xs_ref.at[pl.ds(pos_ref[0, k, r] * SLOT_ROWS, SLOT_ROWS)], sem)

    def zero_copy(r):
        return pltpu.make_async_copy(zero_ref, xs_ref.at[pl.ds(r * SLOT_ROWS, SLOT_ROWS)], sem)

    def for_each_padding_row(fn):
        for e in range(N_EXPERTS):
            lax.fori_loop(start_ref[e] + cnt_ref[e], end_ref[e], fn, 0)
        lax.fori_loop(end_ref[N_EXPERTS - 1], rows, fn, 0)

    def start(r, c):
        row_copy(r, 0).start(priority=0)
        row_copy(r, 1).start(priority=1)
        return c

    def wait(r, c):
        row_copy(r, 0).wait()
        row_copy(r, 1).wait()
        return c

    def zero_start(r, c):
        zero_copy(r).start()
        return c

    def zero_wait(r, c):
        zero_copy(r).wait()
        return c

    lax.fori_loop(0, tm, start, 0)

    @pl.when(pl.program_id(0) == 0)
    def _():
        zero_ref[...] = jnp.zeros_like(zero_ref)
        for_each_padding_row(zero_start)
        for_each_padding_row(zero_wait)

    lax.fori_loop(0, tm, wait, 0)


def _moe_dispatch(h, pos_tiles, counts, starts, ends, rows):
    t, d = h.shape
    assert d == V7X_SUBLANES * V7X_LANES, "one token must fill one (8, 128) tile"
    tm = pos_tiles.shape[2]
    return pl.pallas_call(
        _dispatch_kernel,
        grid_spec=pltpu.PrefetchScalarGridSpec(
            num_scalar_prefetch=3,
            grid=(t // tm,),
            in_specs=[pl.BlockSpec((1, TOP_K, tm), lambda i, *_: (i, 0, 0), memory_space=pltpu.SMEM),
                      pl.BlockSpec((tm, d), lambda i, *_: (i, 0))],
            out_specs=pl.BlockSpec(memory_space=pl.ANY),
            scratch_shapes=[pltpu.VMEM((tm * SLOT_ROWS, V7X_LANES), F32),
                            pltpu.VMEM((SLOT_ROWS, V7X_LANES), F32),
                            pltpu.SemaphoreType.DMA(())]),
        out_shape=jax.ShapeDtypeStruct((rows * SLOT_ROWS, V7X_LANES), F32),
        compiler_params=_cparams("arbitrary"),
        name="moe_dispatch",
    )(counts, starts, ends, pos_tiles, h)


def _moe_kernel(te_ref, nv_ref, x_ref, wg_ref, wu_ref, wd_ref, o_ref, xb_ref, acc_ref):
    i = pl.program_id(0)
    f = pl.program_id(1)
    last = pl.num_programs(1) - 1
    valid = i < nv_ref[0]
    tg = acc_ref.shape[0]

    @pl.when(valid & (f == 0))
    def _():
        xb_ref[...] = _from_slots(x_ref, tg).astype(BF16)
        acc_ref[...] = jnp.zeros_like(acc_ref)

    @pl.when(valid)
    def _():
        x = xb_ref[...]
        hid = jax.nn.silu(_dot(x, wg_ref[0])) * _dot(x, wu_ref[0])
        acc_ref[...] += _dot(hid.astype(BF16), wd_ref[0])

    @pl.when(valid & (f == last))
    def _():
        _to_slots(o_ref, acc_ref[...])

    @pl.when(jnp.logical_not(valid) & (f == last))
    def _():
        o_ref[...] = jnp.zeros_like(o_ref)


def _moe_experts(xs, tile_expert, n_valid, w_gate, w_up, w_down):
    d = V7X_SUBLANES * V7X_LANES
    r = xs.shape[0] // SLOT_ROWS
    d_ff = w_gate.shape[2]
    tg = MOE_ROW_TILE
    tf = _ffn_chunk(d_ff, MOE_FF_CHUNK)
    nf = d_ff // tf

    def f_eff(i, f, nv):
        return jnp.where(i < nv[0], f, nf - 1)

    row_tile = pl.BlockSpec((tg * SLOT_ROWS, V7X_LANES), lambda i, f, te, nv: (i, 0))
    return pl.pallas_call(
        _moe_kernel,
        grid_spec=pltpu.PrefetchScalarGridSpec(
            num_scalar_prefetch=2,
            grid=(r // tg, nf),
            in_specs=[row_tile,
                      pl.BlockSpec((1, d, tf), lambda i, f, te, nv: (te[i], 0, f_eff(i, f, nv))),
                      pl.BlockSpec((1, d, tf), lambda i, f, te, nv: (te[i], 0, f_eff(i, f, nv))),
                      pl.BlockSpec((1, tf, d), lambda i, f, te, nv: (te[i], f_eff(i, f, nv), 0))],
            out_specs=row_tile,
            scratch_shapes=[pltpu.VMEM((tg, d), BF16), pltpu.VMEM((tg, d), F32)]),
        out_shape=jax.ShapeDtypeStruct(xs.shape, F32),
        compiler_params=_cparams("parallel", "arbitrary"),
        name="moe_experts",
    )(tile_expert, n_valid, xs, w_gate, w_up, w_down)


def _combine_kernel(pos_ref, h_ref, p_ref, y_ref, lng_ref, lnb_ref, o_ref, buf_ref, sem):
    tm = h_ref.shape[0]

    def row_copy(r, k):
        return pltpu.make_async_copy(y_ref.at[pl.ds(pos_ref[0, k, r] * SLOT_ROWS, SLOT_ROWS)],
                                     buf_ref.at[k, pl.ds(r * SLOT_ROWS, SLOT_ROWS)], sem)

    def start(r, c):
        row_copy(r, 0).start(priority=0)
        row_copy(r, 1).start(priority=1)
        return c

    def wait(r, c):
        row_copy(r, 0).wait()
        row_copy(r, 1).wait()
        return c

    lax.fori_loop(0, tm, start, 0)
    lax.fori_loop(0, tm, wait, 0)
    p = p_ref[...]
    ffn = p[:, 0:1] * _from_slots(buf_ref.at[0], tm) + p[:, 1:2] * _from_slots(buf_ref.at[1], tm)
    o_ref[...] = _layer_norm(ALPHA * h_ref[...] + ffn, lng_ref[...], lnb_ref[...])


def _moe_combine(h, top_p, y, pos_tiles, ln_g, ln_b):
    t, d = h.shape
    tm = pos_tiles.shape[2]
    return pl.pallas_call(
        _combine_kernel,
        grid=(t // tm,),
        in_specs=[pl.BlockSpec((1, TOP_K, tm), lambda i: (i, 0, 0), memory_space=pltpu.SMEM),
                  pl.BlockSpec((tm, d), lambda i: (i, 0)),
                  pl.BlockSpec((tm, ROUTER_LANES), lambda i: (i, 0)),
                  pl.BlockSpec(memory_space=pl.ANY),
                  pl.BlockSpec((1, d), lambda i: (0, 0)),
                  pl.BlockSpec((1, d), lambda i: (0, 0))],
        out_specs=pl.BlockSpec((tm, d), lambda i: (i, 0)),
        out_shape=jax.ShapeDtypeStruct((t, d), F32),
        scratch_shapes=[pltpu.VMEM((TOP_K, tm * SLOT_ROWS, V7X_LANES), F32),
                        pltpu.SemaphoreType.DMA(())],
        compiler_params=_cparams("arbitrary"),
        name="moe_combine",
    )(pos_tiles, h, top_p, y, ln_g.reshape(1, -1), ln_b.reshape(1, -1))


def _route_plan(top_idx, t):
    tg = MOE_ROW_TILE
    e_flat = top_idx.T.reshape(-1)
    onehot = (e_flat[:, None] == jnp.arange(N_EXPERTS)[None, :]).astype(jnp.int32)
    csum = jnp.cumsum(onehot, axis=0)
    counts = csum[-1]
    padded = (counts + tg - 1) // tg * tg
    ends = jnp.cumsum(padded)
    starts = ends - padded
    pos = jnp.sum(onehot * (csum - 1 + starts[None, :]), axis=1)
    rows = TOP_K * t + N_EXPERTS * tg
    tile_start = jnp.arange(rows // tg, dtype=jnp.int32) * tg
    tile_expert = jnp.sum((ends[None, :] <= tile_start[:, None]).astype(jnp.int32), axis=1)
    tile_expert = jnp.minimum(tile_expert, N_EXPERTS - 1)
    n_valid = (ends[N_EXPERTS - 1] // tg).reshape(1)
    tm = GATHER_TILE
    pos_tiles = jnp.transpose(pos.reshape(TOP_K, t // tm, tm), (1, 0, 2))
    return pos_tiles, counts, starts, ends, tile_expert, n_valid, rows


def _moe(h, top_idx, top_p, w_gate, w_up, w_down, ln_g, ln_b):
    pos_tiles, counts, starts, ends, tile_expert, n_valid, rows = _route_plan(top_idx, h.shape[0])
    xs = _moe_dispatch(h, pos_tiles, counts, starts, ends, rows)
    y = _moe_experts(xs, tile_expert, n_valid,
                     w_gate.astype(BF16), w_up.astype(BF16), w_down.astype(BF16))
    return _moe_combine(h, top_p, y, pos_tiles, ln_g, ln_b)


_DIM_MAJOR = (np.arange(B_WIDTH) % B_HEADS) * HEAD_DIM + np.arange(B_WIDTH) // B_HEADS


def _dim_major_params(w_in, mu, w0, w_up, a0, a_up, g_up, k_k, k_a, r_k, gn_g, gn_b, w_out):
    pm = _DIM_MAJOR
    cols = np.concatenate([pm, B_WIDTH + pm, 2 * B_WIDTH + pm, np.arange(3 * B_WIDTH, B_IN_WIDTH)])
    w_in = jnp.concatenate([w_in[:, :QKV_WIDTH], w_in[:, QKV_WIDTH + cols]], axis=1)
    w_out = jnp.concatenate([w_out[:A_WIDTH], w_out[A_WIDTH + pm]], axis=0)
    return (w_in, mu[cols], w0[pm], w_up[:, pm], a0[pm], a_up[:, pm], g_up[:, pm], k_k[pm], k_a[pm],
            r_k.reshape(-1)[pm], gn_g[pm], gn_b[pm], w_out)


def _v_to_scan_layout(v, bsz, seq):
    half = HEAD_DIM // 2
    v = v.reshape(bsz, seq, 2, half, B_HEADS)
    return jnp.transpose(v, (1, 3, 2, 0, 4)).reshape(seq, half, V7X_LANES)


def _y_from_scan_layout(y, bsz, seq):
    half = HEAD_DIM // 2
    y = y.reshape(seq, half, 2, bsz, B_HEADS)
    return jnp.transpose(y, (3, 0, 2, 1, 4)).reshape(bsz * seq, B_WIDTH)


def kernel(x, rel_bias_table, even_w_in, even_sinks, rwkv_mu, rwkv_w0, rwkv_w_up, rwkv_a0, rwkv_a_up, rwkv_g_up, rwkv_k_k, rwkv_k_a, rwkv_r_k, rwkv_gn_g, rwkv_gn_b, even_w_out, even_ln_mix_g, even_ln_mix_b, dense_w_gate, dense_w_up, dense_w_down, even_ln_ffn_g, even_ln_ffn_b, odd_w_in, odd_conv_w, odd_w_out, odd_ln_mix_g, odd_ln_mix_b, router_w, moe_w_gate, moe_w_up, moe_w_down, odd_ln_ffn_g, odd_ln_ffn_b):
    bsz, seq, d = x.shape
    assert seq % TOKEN_TILE == 0 and seq % BLOCK == 0 and seq % SCAN_CHUNK == 0
    assert bsz * B_HEADS * 2 == V7X_LANES, "scan layout puts (half, batch, head) on the lanes"
    h = x.reshape(bsz * seq, d)
    ones = jnp.asarray(np.kron(np.ones((HEAD_DIM, HEAD_DIM)), np.eye(B_HEADS)), BF16)
    bias = _rel_bias(rel_bias_table)
    for layer in range(DEPTH):
        i = layer // 2
        if layer % 2 == 0:
            (w_in, mu, w0, w_up, a0, a_up, g_up, k_k, k_a, r_k, gn_g, gn_b, w_out) = _dim_major_params(
                even_w_in[i], rwkv_mu[i], rwkv_w0[i], rwkv_w_up[i], rwkv_a0[i], rwkv_a_up[i],
                rwkv_g_up[i], rwkv_k_k[i], rwkv_k_a[i], rwkv_r_k[i], rwkv_gn_g[i], rwkv_gn_b[i],
                even_w_out[i])
            qkv, zb = _in_proj(h, w_in.astype(BF16))
            ya = _attention(qkv, even_sinks[i], bias, bsz, seq)
            p, v, bonus, g = _rwkv_prep(zb, seq, mu, w0, w_up, a0, a_up, g_up, k_k, k_a, r_k, ones)
            y = _rwkv_scan(p, _v_to_scan_layout(v, bsz, seq), bsz)
            y = _y_from_scan_layout(y, bsz, seq)
            h = _mix_out(ya, y, bonus, g, h, gn_g, gn_b, ones, w_out,
                         even_ln_mix_g[i], even_ln_mix_b[i])
            h = _dense_ffn(h, dense_w_gate[i], dense_w_up[i], dense_w_down[i],
                           even_ln_ffn_g[i], even_ln_ffn_b[i])
        else:
            h, top_idx, top_p = _conv_mixer(h, seq, odd_w_in[i], odd_conv_w[i], odd_w_out[i],
                                            odd_ln_mix_g[i], odd_ln_mix_b[i], router_w[i])
            h = _moe(h, top_idx, top_p, moe_w_gate[i], moe_w_up[i], moe_w_down[i],
                     odd_ln_ffn_g[i], odd_ln_ffn_b[i])
    return h.reshape(bsz, seq, d)
```

```python
import functools
import math

import numpy as np
import jax
import jax.numpy as jnp
from jax import lax
from jax.experimental import pallas as pl
from jax.experimental.pallas import tpu as pltpu

F32 = jnp.float32
BF16 = jnp.bfloat16

HEAD_DIM = 64
A_Q_HEADS = 8
A_KV_HEADS = 2
A_GROUP = A_Q_HEADS // A_KV_HEADS
WINDOW = 128
BLOCK = 128
NUM_BUCKETS = 32
MAX_DISTANCE = 128
B_HEADS = 8
A_WIDTH = A_Q_HEADS * HEAD_DIM
A_KV_WIDTH = A_KV_HEADS * HEAD_DIM
QKV_WIDTH = A_WIDTH + 2 * A_KV_WIDTH
B_WIDTH = B_HEADS * HEAD_DIM
DECAY_LORA = 64
AAA_LORA = 64
GATE_LORA = 128
B_IN_WIDTH = 3 * B_WIDTH + DECAY_LORA + AAA_LORA + GATE_LORA
CONV_WIDTH = 3
N_EXPERTS = 8
TOP_K = 2
DEPTH = 2
ALPHA = (2.0 * DEPTH) ** 0.25
LN_EPS = 1e-5
GN_EPS = 64e-5
NEG_INF = -1e30
ATTN_SCALE = HEAD_DIM ** -0.5

V7X_LANES = 128
V7X_SUBLANES = 8
V7X_MXU_DIM = 256
V7X_VMEM_LIMIT = 56 * 1024 * 1024
DENSE_FF_CHUNK = 4096
MOE_FF_CHUNK = 1792

TOKEN_TILE = 512
SCAN_CHUNK = 128
N_SCAN_VECS = 6
N_KEY_VECS = 5
SLAB_ROWS = SCAN_CHUNK + 8
MOE_ROW_TILE = 512
GATHER_TILE = 256
ROUTER_LANES = 128
SLOT_ROWS = 9


def _cparams(*sem):
    return pltpu.CompilerParams(dimension_semantics=sem, vmem_limit_bytes=V7X_VMEM_LIMIT)


def _layer_norm(y, g, b):
    mu = jnp.mean(y, axis=-1, keepdims=True)
    d = y - mu
    var = jnp.mean(d * d, axis=-1, keepdims=True)
    return d * lax.rsqrt(var + LN_EPS) * g + b


def _dot(a, b):
    return jnp.dot(a, b, preferred_element_type=F32)


def _head_sum(x, ones_blockdiag):
    hi = x.astype(BF16)
    lo = (x - hi.astype(F32)).astype(BF16)
    return _dot(hi, ones_blockdiag) + _dot(lo, ones_blockdiag)


def _in_proj_kernel(x_ref, w_ref, qkv_ref, zb_ref):
    z = _dot(x_ref[...].astype(BF16), w_ref[...])
    qkv_ref[...] = z[:, :QKV_WIDTH].astype(BF16)
    zb_ref[...] = z[:, QKV_WIDTH:]


def _in_proj(h, w):
    t, d = h.shape
    n = w.shape[1]
    tm = TOKEN_TILE
    return pl.pallas_call(
        _in_proj_kernel,
        grid=(t // tm,),
        in_specs=[pl.BlockSpec((tm, d), lambda i: (i, 0)),
                  pl.BlockSpec((d, n), lambda i: (0, 0))],
        out_specs=[pl.BlockSpec((tm, QKV_WIDTH), lambda i: (i, 0)),
                   pl.BlockSpec((tm, n - QKV_WIDTH), lambda i: (i, 0))],
        out_shape=[jax.ShapeDtypeStruct((t, QKV_WIDTH), BF16),
                   jax.ShapeDtypeStruct((t, n - QKV_WIDTH), F32)],
        compiler_params=_cparams("parallel"),
        name="in_proj",
    )(h, w)


def _bucket_table():
    qi = np.arange(BLOCK)[:, None]
    ki = np.arange(2 * BLOCK)[None, :]
    n = np.maximum(qi + BLOCK - ki, 0)
    max_exact = NUM_BUCKETS // 2
    log_ratio = (np.log(np.maximum(n, 1).astype(np.float32) / max_exact)
                 / math.log(MAX_DISTANCE / max_exact))
    large = max_exact + (log_ratio * (NUM_BUCKETS - max_exact)).astype(np.int32)
    large = np.minimum(large, NUM_BUCKETS - 1)
    return np.where(n < max_exact, n, large).astype(np.int32)


def _bias_kernel(tab_ref, bucket_ref, out_ref):
    bucket = bucket_ref[...]
    qi = lax.broadcasted_iota(jnp.int32, bucket.shape, 0)
    ki = lax.broadcasted_iota(jnp.int32, bucket.shape, 1)
    dist = qi + BLOCK - ki
    in_window = (dist >= 0) & (dist < WINDOW)
    for h in range(A_Q_HEADS):
        acc = jnp.zeros(bucket.shape, F32)
        for b in range(NUM_BUCKETS):
            acc = jnp.where(bucket == b, tab_ref[b, h], acc)
        out_ref[0, h] = jnp.where(in_window & (ki >= BLOCK), acc, NEG_INF)
        out_ref[1, h] = jnp.where(in_window, acc, NEG_INF)


def _rel_bias(table):
    return pl.pallas_call(
        _bias_kernel,
        in_specs=[pl.BlockSpec(memory_space=pltpu.SMEM),
                  pl.BlockSpec(memory_space=pltpu.VMEM)],
        out_specs=pl.BlockSpec(memory_space=pltpu.VMEM),
        out_shape=jax.ShapeDtypeStruct((2, A_Q_HEADS, BLOCK, 2 * BLOCK), F32),
        name="rel_bias",
    )(table, jnp.asarray(_bucket_table()))


def _attn_kernel(sink_ref, q_ref, kp_ref, kc_ref, vp_ref, vc_ref, bias_ref, o_ref):
    q = q_ref[...] * ATTN_SCALE
    kcat = jnp.concatenate([kp_ref[...], kc_ref[...]], axis=0)
    vcat = jnp.concatenate([vp_ref[...], vc_ref[...]], axis=0)
    outs = []
    for hk in range(A_KV_HEADS):
        k_h = kcat[:, hk * HEAD_DIM:(hk + 1) * HEAD_DIM]
        v_h = vcat[:, hk * HEAD_DIM:(hk + 1) * HEAD_DIM]
        for g in range(A_GROUP):
            hq = hk * A_GROUP + g
            q_h = q[:, hq * HEAD_DIM:(hq + 1) * HEAD_DIM]
            s = lax.dot_general(q_h, k_h, (((1,), (1,)), ((), ())),
                                preferred_element_type=F32) + bias_ref[0, hq]
            sink = sink_ref[hq]
            m = jnp.maximum(jnp.max(s, axis=-1, keepdims=True), sink)
            p = jnp.exp(s - m)
            denom = jnp.sum(p, axis=-1, keepdims=True) + jnp.exp(sink - m)
            o = _dot(p.astype(BF16), v_h)
            outs.append(o / denom)
    o_ref[...] = jnp.concatenate(outs, axis=-1).astype(o_ref.dtype)


def _attention(qkv, sinks, bias, bsz, seq):
    t = qkv.shape[0]
    nb = seq // BLOCK
    kcol = A_WIDTH // A_KV_WIDTH
    vcol = kcol + 1

    def cur(b, n):
        return b * nb + n

    def prev(b, n):
        return b * nb + jnp.maximum(n - 1, 0)

    return pl.pallas_call(
        _attn_kernel,
        grid=(bsz, nb),
        in_specs=[pl.BlockSpec(memory_space=pltpu.SMEM),
                  pl.BlockSpec((BLOCK, A_WIDTH), lambda b, n: (cur(b, n), 0)),
                  pl.BlockSpec((BLOCK, A_KV_WIDTH), lambda b, n: (prev(b, n), kcol)),
                  pl.BlockSpec((BLOCK, A_KV_WIDTH), lambda b, n: (cur(b, n), kcol)),
                  pl.BlockSpec((BLOCK, A_KV_WIDTH), lambda b, n: (prev(b, n), vcol)),
                  pl.BlockSpec((BLOCK, A_KV_WIDTH), lambda b, n: (cur(b, n), vcol)),
                  pl.BlockSpec((1, A_Q_HEADS, BLOCK, 2 * BLOCK),
                               lambda b, n: (jnp.minimum(n, 1), 0, 0, 0))],
        out_specs=pl.BlockSpec((BLOCK, A_WIDTH), lambda b, n: (cur(b, n), 0)),
        out_shape=jax.ShapeDtypeStruct((t, A_WIDTH), BF16),
        compiler_params=_cparams("parallel", "parallel"),
        name="swa_attention",
    )(sinks, qkv, qkv, qkv, qkv, qkv, bias)


def _rwkv_prep_kernel(seq, zb_ref, halo_ref, mu_ref, w0_ref, wup_ref, a0_ref, aup_ref,
                      gup_ref, kk_ref, ka_ref, rk_ref, ones_ref,
                      p_ref, bonus_ref, g_ref):
    tm = zb_ref.shape[0]
    i = pl.program_id(0)
    z = zb_ref[...]
    seq_start = (i * tm) % seq == 0
    prev_last = jnp.where(seq_start, 0.0, halo_ref[V7X_SUBLANES - 1:V7X_SUBLANES, :])
    row = lax.broadcasted_iota(jnp.int32, z.shape, 0)
    zsh = jnp.where(row == 0, prev_last, pltpu.roll(z, 1, axis=0))
    zs = z + (zsh - z) * mu_ref[...]
    c = B_WIDTH
    zr, zk, zv = zs[:, :c], zs[:, c:2 * c], zs[:, 2 * c:3 * c]
    zwd = zs[:, 3 * c:3 * c + DECAY_LORA]
    zad = zs[:, 3 * c + DECAY_LORA:3 * c + DECAY_LORA + AAA_LORA]
    zgd = zs[:, 3 * c + DECAY_LORA + AAA_LORA:]
    ones = ones_ref[...]

    w = -jax.nn.softplus(-(w0_ref[...] + _dot(jnp.tanh(zwd).astype(BF16), wup_ref[...]))) - 0.5
    decay = jnp.exp(-jnp.exp(w))
    a = jax.nn.sigmoid(a0_ref[...] + _dot(zad.astype(BF16), aup_ref[...]))
    g = _dot(jax.nn.sigmoid(zgd).astype(BF16), gup_ref[...])

    kk = zk * kk_ref[...]
    kk = kk / jnp.maximum(jnp.sqrt(_head_sum(kk * kk, ones)), 1e-12)
    k = zk * (1.0 + (a - 1.0) * ka_ref[...])
    p_ref[0] = zr.T
    p_ref[1] = decay.T
    p_ref[2] = k.T
    p_ref[3] = (-kk).T
    p_ref[4] = (kk * a).T
    p_ref[5] = zv.T
    bonus_ref[...] = _head_sum(zr * k * rk_ref[...], ones) * zv
    g_ref[...] = g


def _rwkv_prep(zb, seq, mu, w0, w_up, a0, a_up, g_up, k_k, k_a, r_k, ones):
    t, n = zb.shape
    tm = TOKEN_TILE
    hb = tm // V7X_SUBLANES
    c = B_WIDTH

    def row(v):
        return v.reshape(1, -1)

    def full(a):
        return pl.BlockSpec(a.shape, lambda i: (0,) * a.ndim)

    small = [row(mu), row(w0), w_up.astype(BF16), row(a0), a_up.astype(BF16),
             g_up.astype(BF16), row(k_k), row(k_a), row(r_k), ones]
    tok = pl.BlockSpec((tm, c), lambda i: (i, 0))
    return pl.pallas_call(
        functools.partial(_rwkv_prep_kernel, seq),
        grid=(t // tm,),
        in_specs=[pl.BlockSpec((tm, n), lambda i: (i, 0)),
                  pl.BlockSpec((V7X_SUBLANES, n), lambda i: (jnp.maximum(i * hb - 1, 0), 0))]
                 + [full(a) for a in small],
        out_specs=[pl.BlockSpec((N_SCAN_VECS, c, tm), lambda i: (0, 0, i)), tok, tok],
        out_shape=[jax.ShapeDtypeStruct((N_SCAN_VECS, c, t), F32)]
                  + [jax.ShapeDtypeStruct((t, c), F32)] * 2,
        compiler_params=_cparams("parallel"),
        name="rwkv_prep",
    )(zb, zb, *small)


def _rwkv_scan_kernel(p_hbm, y_hbm, s_ref, pin_ref, yout_ref, slab_ref, sem):
    c = pl.program_id(0)
    nc = pl.num_programs(0)
    n_b = pin_ref.shape[0]
    n_steps = pin_ref.shape[3]
    half = HEAD_DIM // 2
    v_base = HEAD_DIM * N_KEY_VECS
    y_base = v_base + half

    def time_chunk(chunk, b):
        return pl.ds(pl.multiple_of((b * nc + chunk) * n_steps, n_steps), n_steps)

    def in_copy(chunk, b):
        return pltpu.make_async_copy(p_hbm.at[:, :, time_chunk(chunk, b)], pin_ref.at[b], sem.at[0, b])

    def out_copy(chunk, b):
        return pltpu.make_async_copy(yout_ref.at[b], y_hbm.at[:, time_chunk(chunk, b)], sem.at[1, b])

    def slab(index):
        return pl.ds(pl.multiple_of(index * SLAB_ROWS, V7X_SUBLANES), n_steps)

    def feature_rows(dim):
        return pl.ds(pl.multiple_of(dim * B_HEADS, B_HEADS), B_HEADS)

    @pl.when(c == 0)
    def _():
        s_ref[...] = jnp.zeros_like(s_ref)
        for b in range(n_b):
            in_copy(0, b).start()

    for b in range(n_b):
        in_copy(c, b).wait()

    def relayout_keys(j, carry):
        for which in range(N_KEY_VECS):
            x = jnp.concatenate([pin_ref[b, which, feature_rows(j), :] for b in range(n_b)] * 2,
                                axis=0)
            slab_ref[slab(j * N_KEY_VECS + which), :] = x.T
        return carry

    def relayout_values(i, carry):
        x = jnp.concatenate([pin_ref[b, N_KEY_VECS, feature_rows(ih * half + i), :]
                             for ih in range(2) for b in range(n_b)], axis=0)
        slab_ref[slab(v_base + i), :] = x.T
        return carry

    lax.fori_loop(0, HEAD_DIM, relayout_keys, 0, unroll=2)
    lax.fori_loop(0, half, relayout_values, 0, unroll=4)

    @pl.when(c + 1 < nc)
    def _():
        for b in range(n_b):
            in_copy(c + 1, b).start()

    def vec(t, which, j):
        return slab_ref[pl.ds((j * N_KEY_VECS + which) * SLAB_ROWS + t, 1), :]

    def step_rows(base, t):
        return pl.ds(base * SLAB_ROWS + t, half, stride=SLAB_ROWS)

    zero = jnp.zeros((half, V7X_LANES), F32)

    sa_first = [zero, zero]
    for j in range(HEAD_DIM):
        sa_first[j % 2] = sa_first[j % 2] + s_ref[j] * vec(0, 3, j)

    def step(t, sa):
        v_t = slab_ref[step_rows(v_base, t), :]
        t_next = jnp.minimum(t + 1, n_steps - 1)
        y = [zero, zero]
        sa_next = [zero, zero]
        for j in range(HEAD_DIM):
            s = s_ref[j] * vec(t, 1, j) + sa * vec(t, 4, j) + v_t * vec(t, 2, j)
            s_ref[j] = s
            y[j % 2] = y[j % 2] + s * vec(t, 0, j)
            sa_next[j % 2] = sa_next[j % 2] + s * vec(t_next, 3, j)
        slab_ref[step_rows(y_base, t), :] = y[0] + y[1]
        return sa_next[0] + sa_next[1]

    lax.fori_loop(0, n_steps, step, sa_first[0] + sa_first[1])

    @pl.when(c > 0)
    def _():
        for b in range(n_b):
            out_copy(c - 1, b).wait()

    def write_y(i, carry):
        yt = slab_ref[slab(y_base + i), :].T
        for ih in range(2):
            for b in range(n_b):
                r = (ih * n_b + b) * B_HEADS
                yout_ref[b, feature_rows(ih * half + i), :] = yt[r:r + B_HEADS, :]
        return carry

    lax.fori_loop(0, half, write_y, 0, unroll=4)
    for b in range(n_b):
        out_copy(c, b).start()

    @pl.when(c == nc - 1)
    def _():
        for b in range(n_b):
            out_copy(c, b).wait()


def _rwkv_scan(p, bsz):
    n_vec, c, t = p.shape
    seq = t // bsz
    tt = SCAN_CHUNK
    half = HEAD_DIM // 2
    n_slabs = HEAD_DIM * N_KEY_VECS + 2 * half
    return pl.pallas_call(
        _rwkv_scan_kernel,
        grid=(seq // tt,),
        in_specs=[pl.BlockSpec(memory_space=pl.ANY)],
        out_specs=pl.BlockSpec(memory_space=pl.ANY),
        out_shape=jax.ShapeDtypeStruct((c, t), F32),
        scratch_shapes=[pltpu.VMEM((HEAD_DIM, half, V7X_LANES), F32),
                        pltpu.VMEM((bsz, n_vec, c, tt), F32),
                        pltpu.VMEM((bsz, c, tt), F32),
                        pltpu.VMEM((n_slabs * SLAB_ROWS, V7X_LANES), F32),
                        pltpu.SemaphoreType.DMA((2, bsz))],
        compiler_params=_cparams("arbitrary"),
        name="rwkv_scan",
    )(p)


def _mix_out_kernel(ya_ref, y_ref, bonus_ref, g_ref, h_ref, gng_ref, gnb_ref, ones_ref,
                    wa_ref, wb_ref, lng_ref, lnb_ref, o_ref):
    ones = ones_ref[...]
    y = y_ref[...].T
    inv = 1.0 / HEAD_DIM
    d = y - _head_sum(y, ones) * inv
    var = _head_sum(d * d, ones) * inv
    yn = d * lax.rsqrt(var + GN_EPS) * gng_ref[...] + gnb_ref[...]
    yb = (yn + bonus_ref[...]) * g_ref[...]
    mix = _dot(ya_ref[...], wa_ref[...]) + _dot(yb.astype(BF16), wb_ref[...])
    o_ref[...] = _layer_norm(ALPHA * h_ref[...] + mix, lng_ref[...], lnb_ref[...])


def _mix_out(ya, y, bonus, g, h, gn_g, gn_b, ones, w_out, ln_g, ln_b):
    t, d = h.shape
    tm = TOKEN_TILE
    c = B_WIDTH
    wa = w_out[:A_WIDTH].astype(BF16)
    wb = w_out[A_WIDTH:].astype(BF16)
    small = [gn_g.reshape(1, -1), gn_b.reshape(1, -1), ones, wa, wb,
             ln_g.reshape(1, -1), ln_b.reshape(1, -1)]
    tok = pl.BlockSpec((tm, c), lambda i: (i, 0))
    return pl.pallas_call(
        _mix_out_kernel,
        grid=(t // tm,),
        in_specs=[tok, pl.BlockSpec((c, tm), lambda i: (0, i)), tok, tok,
                  pl.BlockSpec((tm, d), lambda i: (i, 0))]
                 + [pl.BlockSpec(a.shape, lambda i: (0, 0)) for a in small],
        out_specs=pl.BlockSpec((tm, d), lambda i: (i, 0)),
        out_shape=jax.ShapeDtypeStruct((t, d), F32),
        compiler_params=_cparams("parallel"),
        name="mix_out",
    )(ya, y, bonus, g, h, *small)


def _ffn_kernel(x_ref, wg_ref, wu_ref, wd_ref, lng_ref, lnb_ref, o_ref, xb_ref, acc_ref):
    f = pl.program_id(1)

    @pl.when(f == 0)
    def _():
        xb_ref[...] = x_ref[...].astype(BF16)
        acc_ref[...] = jnp.zeros_like(acc_ref)

    xb = xb_ref[...]
    hid = jax.nn.silu(_dot(xb, wg_ref[...])) * _dot(xb, wu_ref[...])
    acc_ref[...] += _dot(hid.astype(BF16), wd_ref[...])

    @pl.when(f == pl.num_programs(1) - 1)
    def _():
        o_ref[...] = _layer_norm(ALPHA * x_ref[...] + acc_ref[...], lng_ref[...], lnb_ref[...])


def _ffn_chunk(d_ff, limit):
    for unit in (V7X_MXU_DIM, V7X_LANES):
        for n in range(limit // unit, 0, -1):
            if d_ff % (n * unit) == 0:
                return n * unit
    raise ValueError(f"d_ff={d_ff} is not a multiple of {V7X_LANES}")


def _dense_ffn(x, w_gate, w_up, w_down, ln_g, ln_b):
    t, d = x.shape
    d_ff = w_gate.shape[1]
    tm = TOKEN_TILE
    tf = _ffn_chunk(d_ff, DENSE_FF_CHUNK)
    mode = dict(pipeline_mode=pl.Buffered(1)) if tf == d_ff else {}
    return pl.pallas_call(
        _ffn_kernel,
        grid=(t // tm, d_ff // tf),
        in_specs=[pl.BlockSpec((tm, d), lambda i, f: (i, 0)),
                  pl.BlockSpec((d, tf), lambda i, f: (0, f), **mode),
                  pl.BlockSpec((d, tf), lambda i, f: (0, f), **mode),
                  pl.BlockSpec((tf, d), lambda i, f: (f, 0), **mode),
                  pl.BlockSpec((1, d), lambda i, f: (0, 0)),
                  pl.BlockSpec((1, d), lambda i, f: (0, 0))],
        out_specs=pl.BlockSpec((tm, d), lambda i, f: (i, 0)),
        out_shape=jax.ShapeDtypeStruct((t, d), F32),
        scratch_shapes=[pltpu.VMEM((tm, d), BF16), pltpu.VMEM((tm, d), F32)],
        compiler_params=_cparams("parallel", "arbitrary"),
        name="dense_ffn",
    )(x, w_gate.astype(BF16), w_up.astype(BF16), w_down.astype(BF16),
      ln_g.reshape(1, -1), ln_b.reshape(1, -1))


def _conv_mixer_kernel(seq, x_ref, halo_ref, win_ref, cw_ref, wout_ref, lng_ref, lnb_ref, rwt_ref,
                       o_ref, idx_ref, prob_ref):
    tm, d = x_ref.shape
    i = pl.program_id(0)
    x = x_ref[...]
    seq_start = (i * tm) % seq == 0
    xa = jnp.concatenate([halo_ref[...], x], axis=0).astype(BF16)
    z = _dot(xa, win_ref[...])
    gb = z[V7X_SUBLANES:, :d]
    u = z[:, d:2 * d] * z[:, 2 * d:]
    row = lax.broadcasted_iota(jnp.int32, u.shape, 0)
    u = jnp.where((row < V7X_SUBLANES) & seq_start, 0.0, u)
    cw = cw_ref[...]
    conv = pltpu.roll(u, 2, axis=0)[V7X_SUBLANES:] * cw[0:1]
    conv = conv + pltpu.roll(u, 1, axis=0)[V7X_SUBLANES:] * cw[1:2]
    conv = conv + u[V7X_SUBLANES:] * cw[2:3]
    y = _dot((gb * conv).astype(BF16), wout_ref[...])
    h = _layer_norm(ALPHA * x + y, lng_ref[...], lnb_ref[...])
    o_ref[...] = h
    _route_top2(h, rwt_ref, idx_ref, prob_ref)


def _conv_mixer(x, seq, w_in, conv_w, w_out, ln_g, ln_b, router_w):
    t, d = x.shape
    tm = TOKEN_TILE
    hb = tm // V7X_SUBLANES
    small = [w_in.astype(BF16), conv_w, w_out.astype(BF16), ln_g.reshape(1, -1), ln_b.reshape(1, -1),
             router_w.T]
    route = pl.BlockSpec((tm, ROUTER_LANES), lambda i: (i, 0))
    h, idx, prob = pl.pallas_call(
        functools.partial(_conv_mixer_kernel, seq),
        grid=(t // tm,),
        in_specs=[pl.BlockSpec((tm, d), lambda i: (i, 0)),
                  pl.BlockSpec((V7X_SUBLANES, d), lambda i: (jnp.maximum(i * hb - 1, 0), 0))]
                 + [pl.BlockSpec(a.shape, lambda i: (0, 0)) for a in small],
        out_specs=[pl.BlockSpec((tm, d), lambda i: (i, 0)), route, route],
        out_shape=[jax.ShapeDtypeStruct((t, d), F32),
                   jax.ShapeDtypeStruct((t, ROUTER_LANES), jnp.int32),
                   jax.ShapeDtypeStruct((t, ROUTER_LANES), F32)],
        compiler_params=_cparams("parallel"),
        name="conv_mixer",
    )(x, x, *small)
    return h, idx[:, :TOP_K], prob


def _route_top2(x, wt_ref, idx_ref, prob_ref):
    lane = lax.broadcasted_iota(jnp.int32, (x.shape[0], ROUTER_LANES), 1)
    logits = jnp.full((x.shape[0], ROUTER_LANES), -jnp.inf, F32)
    for e in range(N_EXPERTS):
        logit_e = jnp.sum(x * wt_ref[e:e + 1, :], axis=-1, keepdims=True)
        logits = jnp.where(lane == e, logit_e, logits)
    m1 = jnp.max(logits, axis=-1, keepdims=True)
    i1 = jnp.min(jnp.where(logits == m1, lane, ROUTER_LANES), axis=-1, keepdims=True)
    rest = jnp.where(lane == i1, -jnp.inf, logits)
    m2 = jnp.max(rest, axis=-1, keepdims=True)
    i2 = jnp.min(jnp.where(rest == m2, lane, ROUTER_LANES), axis=-1, keepdims=True)
    e2 = jnp.exp(m2 - m1)
    den = 1.0 + e2
    idx_ref[...] = jnp.where(lane == 0, i1, jnp.where(lane == 1, i2, 0))
    prob_ref[...] = jnp.where(lane == 0, 1.0 / den, jnp.where(lane == 1, e2 / den, 0.0))


def _to_slots(slot_ref, x):
    rows = x.shape[0]
    for s in range(V7X_SUBLANES):
        slot_ref[pl.ds(s, rows, stride=SLOT_ROWS), :] = x[:, s * V7X_LANES:(s + 1) * V7X_LANES]
    slot_ref[pl.ds(V7X_SUBLANES, rows, stride=SLOT_ROWS), :] = jnp.zeros((rows, V7X_LANES), x.dtype)


def _from_slots(slot_ref, rows):
    return jnp.concatenate([slot_ref[pl.ds(s, rows, stride=SLOT_ROWS), :] for s in range(V7X_SUBLANES)],
                           axis=-1)


def _dispatch_kernel(cnt_ref, start_ref, end_ref, pos_ref, h_ref, xs_ref, buf_ref, zero_ref, sem):
    tm = h_ref.shape[0]
    rows = xs_ref.shape[0] // SLOT_ROWS
    i = pl.program_id(0)
    cur = i % 2
    zero_sem = 2
    _to_slots(buf_ref.at[cur], h_ref[...])

    def row_copy(r, k):
        return pltpu.make_async_copy(buf_ref.at[cur, pl.ds(r * SLOT_ROWS, SLOT_ROWS)],
                                     xs_ref.at[pl.ds(pos_ref[0, k, r] * SLOT_ROWS, SLOT_ROWS)],
                                     sem.at[cur])

    def wait_tile(which):
        for _ in range(TOP_K):
            pltpu.make_async_copy(buf_ref.at[which], buf_ref.at[which], sem.at[which]).wait()

    def zero_copy(r):
        return pltpu.make_async_copy(zero_ref, xs_ref.at[pl.ds(r * SLOT_ROWS, SLOT_ROWS)],
                                     sem.at[zero_sem])

    def for_each_padding_row(fn):
        for e in range(N_EXPERTS):
            lax.fori_loop(start_ref[e] + cnt_ref[e], end_ref[e], fn, 0)
        lax.fori_loop(end_ref[N_EXPERTS - 1], rows, fn, 0)

    def start(r, c):
        row_copy(r, 0).start(priority=0)
        row_copy(r, 1).start(priority=1)
        return c

    def zero_start(r, c):
        zero_copy(r).start()
        return c

    def zero_wait(r, c):
        zero_copy(r).wait()
        return c

    lax.fori_loop(0, tm, start, 0, unroll=8)

    @pl.when(i == 0)
    def _():
        zero_ref[...] = jnp.zeros_like(zero_ref)
        for_each_padding_row(zero_start)
        for_each_padding_row(zero_wait)

    @pl.when(i > 0)
    def _():
        wait_tile(1 - cur)

    @pl.when(i == pl.num_programs(0) - 1)
    def _():
        wait_tile(cur)


def _moe_dispatch(h, pos_tiles, counts, starts, ends, rows):
    t, d = h.shape
    assert d == V7X_SUBLANES * V7X_LANES, "one token must fill one (8, 128) tile"
    tm = pos_tiles.shape[2]
    return pl.pallas_call(
        _dispatch_kernel,
        grid_spec=pltpu.PrefetchScalarGridSpec(
            num_scalar_prefetch=3,
            grid=(t // tm,),
            in_specs=[pl.BlockSpec((1, TOP_K, tm), lambda i, *_: (i, 0, 0), memory_space=pltpu.SMEM),
                      pl.BlockSpec((tm, d), lambda i, *_: (i, 0))],
            out_specs=pl.BlockSpec(memory_space=pl.ANY),
            scratch_shapes=[pltpu.VMEM((2, tm * SLOT_ROWS, V7X_LANES), F32),
                            pltpu.VMEM((SLOT_ROWS, V7X_LANES), F32),
                            pltpu.SemaphoreType.DMA((3,))]),
        out_shape=jax.ShapeDtypeStruct((rows * SLOT_ROWS, V7X_LANES), F32),
        compiler_params=_cparams("arbitrary"),
        name="moe_dispatch",
    )(counts, starts, ends, pos_tiles, h)


def _moe_kernel(te_ref, nv_ref, x_ref, wg_ref, wu_ref, wd_ref, o_ref, xb_ref, acc_ref):
    i = pl.program_id(0)
    f = pl.program_id(1)
    last = pl.num_programs(1) - 1
    valid = i < nv_ref[0]
    tg = acc_ref.shape[0]

    @pl.when(valid & (f == 0))
    def _():
        xb_ref[...] = _from_slots(x_ref, tg).astype(BF16)
        acc_ref[...] = jnp.zeros_like(acc_ref)

    @pl.when(valid)
    def _():
        x = xb_ref[...]
        hid = jax.nn.silu(_dot(x, wg_ref[0])) * _dot(x, wu_ref[0])
        acc_ref[...] += _dot(hid.astype(BF16), wd_ref[0])

    @pl.when(valid & (f == last))
    def _():
        _to_slots(o_ref, acc_ref[...])

    @pl.when(jnp.logical_not(valid) & (f == last))
    def _():
        o_ref[...] = jnp.zeros_like(o_ref)


def _moe_experts(xs, tile_expert, n_valid, w_gate, w_up, w_down):
    d = V7X_SUBLANES * V7X_LANES
    r = xs.shape[0] // SLOT_ROWS
    d_ff = w_gate.shape[2]
    tg = MOE_ROW_TILE
    tf = _ffn_chunk(d_ff, MOE_FF_CHUNK)
    nf = d_ff // tf

    def f_eff(i, f, nv):
        return jnp.where(i < nv[0], f, nf - 1)

    row_tile = pl.BlockSpec((tg * SLOT_ROWS, V7X_LANES), lambda i, f, te, nv: (i, 0))
    return pl.pallas_call(
        _moe_kernel,
        grid_spec=pltpu.PrefetchScalarGridSpec(
            num_scalar_prefetch=2,
            grid=(r // tg, nf),
            in_specs=[row_tile,
                      pl.BlockSpec((1, d, tf), lambda i, f, te, nv: (te[i], 0, f_eff(i, f, nv))),
                      pl.BlockSpec((1, d, tf), lambda i, f, te, nv: (te[i], 0, f_eff(i, f, nv))),
                      pl.BlockSpec((1, tf, d), lambda i, f, te, nv: (te[i], f_eff(i, f, nv), 0))],
            out_specs=row_tile,
            scratch_shapes=[pltpu.VMEM((tg, d), BF16), pltpu.VMEM((tg, d), F32)]),
        out_shape=jax.ShapeDtypeStruct(xs.shape, F32),
        compiler_params=_cparams("parallel", "arbitrary"),
        name="moe_experts",
    )(tile_expert, n_valid, xs, w_gate, w_up, w_down)


def _combine_kernel(pos_ref, pos_next_ref, h_ref, p_ref, y_ref, lng_ref, lnb_ref, o_ref, buf_ref, sem):
    tm = h_ref.shape[0]
    i = pl.program_id(0)
    cur = i % 2

    def fetch_tile(tile_pos_ref, which):
        def row_copy(r, k):
            return pltpu.make_async_copy(
                y_ref.at[pl.ds(tile_pos_ref[0, k, r] * SLOT_ROWS, SLOT_ROWS)],
                buf_ref.at[which, k, pl.ds(r * SLOT_ROWS, SLOT_ROWS)], sem.at[which])

        def start(r, c):
            row_copy(r, 0).start(priority=0)
            row_copy(r, 1).start(priority=1)
            return c

        lax.fori_loop(0, tm, start, 0, unroll=8)

    @pl.when(i == 0)
    def _():
        fetch_tile(pos_ref, cur)

    @pl.when(i + 1 < pl.num_programs(0))
    def _():
        fetch_tile(pos_next_ref, 1 - cur)

    pltpu.make_async_copy(buf_ref.at[cur], buf_ref.at[cur], sem.at[cur]).wait()
    p = p_ref[...]
    ffn = (p[:, 0:1] * _from_slots(buf_ref.at[cur, 0], tm)
           + p[:, 1:2] * _from_slots(buf_ref.at[cur, 1], tm))
    o_ref[...] = _layer_norm(ALPHA * h_ref[...] + ffn, lng_ref[...], lnb_ref[...])


def _moe_combine(h, top_p, y, pos_tiles, ln_g, ln_b):
    t, d = h.shape
    tm = pos_tiles.shape[2]
    n_tiles = t // tm
    return pl.pallas_call(
        _combine_kernel,
        grid=(n_tiles,),
        in_specs=[pl.BlockSpec((1, TOP_K, tm), lambda i: (i, 0, 0), memory_space=pltpu.SMEM),
                  pl.BlockSpec((1, TOP_K, tm), lambda i: (jnp.minimum(i + 1, n_tiles - 1), 0, 0),
                               memory_space=pltpu.SMEM),
                  pl.BlockSpec((tm, d), lambda i: (i, 0)),
                  pl.BlockSpec((tm, ROUTER_LANES), lambda i: (i, 0)),
                  pl.BlockSpec(memory_space=pl.ANY),
                  pl.BlockSpec((1, d), lambda i: (0, 0)),
                  pl.BlockSpec((1, d), lambda i: (0, 0))],
        out_specs=pl.BlockSpec((tm, d), lambda i: (i, 0)),
        out_shape=jax.ShapeDtypeStruct((t, d), F32),
        scratch_shapes=[pltpu.VMEM((2, TOP_K, tm * SLOT_ROWS, V7X_LANES), F32),
                        pltpu.SemaphoreType.DMA((2,))],
        compiler_params=_cparams("arbitrary"),
        name="moe_combine",
    )(pos_tiles, pos_tiles, h, top_p, y, ln_g.reshape(1, -1), ln_b.reshape(1, -1))


def _route_plan(top_idx, t):
    tg = MOE_ROW_TILE
    e_flat = top_idx.T.reshape(-1)
    onehot = (e_flat[:, None] == jnp.arange(N_EXPERTS)[None, :]).astype(jnp.int32)
    csum = jnp.cumsum(onehot, axis=0)
    counts = csum[-1]
    padded = (counts + tg - 1) // tg * tg
    ends = jnp.cumsum(padded)
    starts = ends - padded
    pos = jnp.sum(onehot * (csum - 1 + starts[None, :]), axis=1)
    rows = TOP_K * t + N_EXPERTS * tg
    tile_start = jnp.arange(rows // tg, dtype=jnp.int32) * tg
    tile_expert = jnp.sum((ends[None, :] <= tile_start[:, None]).astype(jnp.int32), axis=1)
    tile_expert = jnp.minimum(tile_expert, N_EXPERTS - 1)
    n_valid = (ends[N_EXPERTS - 1] // tg).reshape(1)
    tm = GATHER_TILE
    pos_tiles = jnp.transpose(pos.reshape(TOP_K, t // tm, tm), (1, 0, 2))
    return pos_tiles, counts, starts, ends, tile_expert, n_valid, rows


def _moe(h, top_idx, top_p, w_gate, w_up, w_down, ln_g, ln_b):
    pos_tiles, counts, starts, ends, tile_expert, n_valid, rows = _route_plan(top_idx, h.shape[0])
    xs = _moe_dispatch(h, pos_tiles, counts, starts, ends, rows)
    y = _moe_experts(xs, tile_expert, n_valid,
                     w_gate.astype(BF16), w_up.astype(BF16), w_down.astype(BF16))
    return _moe_combine(h, top_p, y, pos_tiles, ln_g, ln_b)


_DIM_MAJOR = (np.arange(B_WIDTH) % B_HEADS) * HEAD_DIM + np.arange(B_WIDTH) // B_HEADS


def _dim_major_params(w_in, mu, w0, w_up, a0, a_up, g_up, k_k, k_a, r_k, gn_g, gn_b, w_out):
    pm = _DIM_MAJOR
    cols = np.concatenate([pm, B_WIDTH + pm, 2 * B_WIDTH + pm, np.arange(3 * B_WIDTH, B_IN_WIDTH)])
    w_in = jnp.concatenate([w_in[:, :QKV_WIDTH], w_in[:, QKV_WIDTH + cols]], axis=1)
    w_out = jnp.concatenate([w_out[:A_WIDTH], w_out[A_WIDTH + pm]], axis=0)
    return (w_in, mu[cols], w0[pm], w_up[:, pm], a0[pm], a_up[:, pm], g_up[:, pm], k_k[pm], k_a[pm],
            r_k.reshape(-1)[pm], gn_g[pm], gn_b[pm], w_out)


def kernel(x, rel_bias_table, even_w_in, even_sinks, rwkv_mu, rwkv_w0, rwkv_w_up, rwkv_a0, rwkv_a_up, rwkv_g_up, rwkv_k_k, rwkv_k_a, rwkv_r_k, rwkv_gn_g, rwkv_gn_b, even_w_out, even_ln_mix_g, even_ln_mix_b, dense_w_gate, dense_w_up, dense_w_down, even_ln_ffn_g, even_ln_ffn_b, odd_w_in, odd_conv_w, odd_w_out, odd_ln_mix_g, odd_ln_mix_b, router_w, moe_w_gate, moe_w_up, moe_w_down, odd_ln_ffn_g, odd_ln_ffn_b):
    bsz, seq, d = x.shape
    assert seq % TOKEN_TILE == 0 and seq % BLOCK == 0 and seq % SCAN_CHUNK == 0
    assert bsz * B_HEADS * 2 == V7X_LANES, "scan layout puts (half, batch, head) on the lanes"
    h = x.reshape(bsz * seq, d)
    ones = jnp.asarray(np.kron(np.ones((HEAD_DIM, HEAD_DIM)), np.eye(B_HEADS)), BF16)
    bias = _rel_bias(rel_bias_table)
    for layer in range(DEPTH):
        i = layer // 2
        if layer % 2 == 0:
            (w_in, mu, w0, w_up, a0, a_up, g_up, k_k, k_a, r_k, gn_g, gn_b, w_out) = _dim_major_params(
                even_w_in[i], rwkv_mu[i], rwkv_w0[i], rwkv_w_up[i], rwkv_a0[i], rwkv_a_up[i],
                rwkv_g_up[i], rwkv_k_k[i], rwkv_k_a[i], rwkv_r_k[i], rwkv_gn_g[i], rwkv_gn_b[i],
                even_w_out[i])
            qkv, zb = _in_proj(h, w_in.astype(BF16))
            ya = _attention(qkv, even_sinks[i], bias, bsz, seq)
            p, bonus, g = _rwkv_prep(zb, seq, mu, w0, w_up, a0, a_up, g_up, k_k, k_a, r_k, ones)
            y = _rwkv_scan(p, bsz)
            h = _mix_out(ya, y, bonus, g, h, gn_g, gn_b, ones, w_out,
                         even_ln_mix_g[i], even_ln_mix_b[i])
            h = _dense_ffn(h, dense_w_gate[i], dense_w_up[i], dense_w_down[i],
                           even_ln_ffn_g[i], even_ln_ffn_b[i])
        else:
            h, top_idx, top_p = _conv_mixer(h, seq, odd_w_in[i], odd_conv_w[i], odd_w_out[i],
                                            odd_ln_mix_g[i], odd_ln_mix_b[i], router_w[i])
            h = _moe(h, top_idx, top_p, moe_w_gate[i], moe_w_up[i], moe_w_down[i],
                     odd_ln_ffn_g[i], odd_ln_ffn_b[i])
    return h.reshape(bsz, seq, d)
```

```python
import functools
import math

import numpy as np
import jax
import jax.numpy as jnp
from jax import lax
from jax.experimental import pallas as pl
from jax.experimental.pallas import tpu as pltpu

F32 = jnp.float32
BF16 = jnp.bfloat16

HEAD_DIM = 64
A_Q_HEADS = 8
A_KV_HEADS = 2
A_GROUP = A_Q_HEADS // A_KV_HEADS
WINDOW = 128
BLOCK = 128
NUM_BUCKETS = 32
MAX_DISTANCE = 128
B_HEADS = 8
A_WIDTH = A_Q_HEADS * HEAD_DIM
A_KV_WIDTH = A_KV_HEADS * HEAD_DIM
QKV_WIDTH = A_WIDTH + 2 * A_KV_WIDTH
B_WIDTH = B_HEADS * HEAD_DIM
DECAY_LORA = 64
AAA_LORA = 64
GATE_LORA = 128
B_IN_WIDTH = 3 * B_WIDTH + DECAY_LORA + AAA_LORA + GATE_LORA
CONV_WIDTH = 3
N_EXPERTS = 8
TOP_K = 2
DEPTH = 2
ALPHA = (2.0 * DEPTH) ** 0.25
LN_EPS = 1e-5
GN_EPS = 64e-5
NEG_INF = -1e30
ATTN_SCALE = HEAD_DIM ** -0.5

V7X_LANES = 128
V7X_SUBLANES = 8
V7X_MXU_DIM = 256
V7X_VMEM_LIMIT = 56 * 1024 * 1024
DENSE_FF_CHUNK = 4096
MOE_FF_CHUNK = 1792

TOKEN_TILE = 512
SCAN_CHUNK = 128
N_SCAN_VECS = 6
N_KEY_VECS = 5
SLAB_ROWS = SCAN_CHUNK + 8
MOE_ROW_TILE = 512
GATHER_TILE = 512
ROUTER_LANES = 128
SLOT_ROWS = 9


def _cparams(*sem):
    return pltpu.CompilerParams(dimension_semantics=sem, vmem_limit_bytes=V7X_VMEM_LIMIT)


def _layer_norm(y, g, b):
    mu = jnp.mean(y, axis=-1, keepdims=True)
    d = y - mu
    var = jnp.mean(d * d, axis=-1, keepdims=True)
    return d * lax.rsqrt(var + LN_EPS) * g + b


def _dot(a, b):
    return jnp.dot(a, b, preferred_element_type=F32)


def _head_sum(x):
    n = x.shape[1] // V7X_LANES
    s = x[:, :V7X_LANES]
    for c in range(1, n):
        s = s + x[:, c * V7X_LANES:(c + 1) * V7X_LANES]
    shift = B_HEADS
    while shift < V7X_LANES:
        s = s + pltpu.roll(s, shift, axis=1)
        shift *= 2
    return jnp.concatenate([s] * n, axis=1)


def _in_proj_kernel(x_ref, w_ref, qkv_ref, zb_ref):
    z = _dot(x_ref[...].astype(BF16), w_ref[...])
    qkv_ref[...] = z[:, :QKV_WIDTH].astype(BF16)
    zb_ref[...] = z[:, QKV_WIDTH:]


def _in_proj(h, w):
    t, d = h.shape
    n = w.shape[1]
    tm = TOKEN_TILE
    return pl.pallas_call(
        _in_proj_kernel,
        grid=(t // tm,),
        in_specs=[pl.BlockSpec((tm, d), lambda i: (i, 0)),
                  pl.BlockSpec((d, n), lambda i: (0, 0))],
        out_specs=[pl.BlockSpec((tm, QKV_WIDTH), lambda i: (i, 0)),
                   pl.BlockSpec((tm, n - QKV_WIDTH), lambda i: (i, 0))],
        out_shape=[jax.ShapeDtypeStruct((t, QKV_WIDTH), BF16),
                   jax.ShapeDtypeStruct((t, n - QKV_WIDTH), F32)],
        compiler_params=_cparams("parallel"),
        name="in_proj",
    )(h, w)


def _bucket_table():
    qi = np.arange(BLOCK)[:, None]
    ki = np.arange(2 * BLOCK)[None, :]
    n = np.maximum(qi + BLOCK - ki, 0)
    max_exact = NUM_BUCKETS // 2
    log_ratio = (np.log(np.maximum(n, 1).astype(np.float32) / max_exact)
                 / math.log(MAX_DISTANCE / max_exact))
    large = max_exact + (log_ratio * (NUM_BUCKETS - max_exact)).astype(np.int32)
    large = np.minimum(large, NUM_BUCKETS - 1)
    return np.where(n < max_exact, n, large).astype(np.int32)


def _bias_kernel(tab_ref, bucket_ref, out_ref):
    bucket = bucket_ref[...]
    qi = lax.broadcasted_iota(jnp.int32, bucket.shape, 0)
    ki = lax.broadcasted_iota(jnp.int32, bucket.shape, 1)
    dist = qi + BLOCK - ki
    in_window = (dist >= 0) & (dist < WINDOW)
    for h in range(A_Q_HEADS):
        acc = jnp.zeros(bucket.shape, F32)
        for b in range(NUM_BUCKETS):
            acc = jnp.where(bucket == b, tab_ref[b, h], acc)
        out_ref[0, h] = jnp.where(in_window & (ki >= BLOCK), acc, NEG_INF)
        out_ref[1, h] = jnp.where(in_window, acc, NEG_INF)


def _rel_bias(table):
    return pl.pallas_call(
        _bias_kernel,
        in_specs=[pl.BlockSpec(memory_space=pltpu.SMEM),
                  pl.BlockSpec(memory_space=pltpu.VMEM)],
        out_specs=pl.BlockSpec(memory_space=pltpu.VMEM),
        out_shape=jax.ShapeDtypeStruct((2, A_Q_HEADS, BLOCK, 2 * BLOCK), F32),
        name="rel_bias",
    )(table, jnp.asarray(_bucket_table()))


def _attn_kernel(sink_ref, q_ref, kp_ref, kc_ref, vp_ref, vc_ref, bias_ref, o_ref):
    q = q_ref[...] * ATTN_SCALE
    kcat = jnp.concatenate([kp_ref[...], kc_ref[...]], axis=0)
    vcat = jnp.concatenate([vp_ref[...], vc_ref[...]], axis=0)
    outs = []
    for hk in range(A_KV_HEADS):
        k_h = kcat[:, hk * HEAD_DIM:(hk + 1) * HEAD_DIM]
        v_h = vcat[:, hk * HEAD_DIM:(hk + 1) * HEAD_DIM]
        for g in range(A_GROUP):
            hq = hk * A_GROUP + g
            q_h = q[:, hq * HEAD_DIM:(hq + 1) * HEAD_DIM]
            s = lax.dot_general(q_h, k_h, (((1,), (1,)), ((), ())),
                                preferred_element_type=F32) + bias_ref[0, hq]
            sink = sink_ref[hq]
            m = jnp.maximum(jnp.max(s, axis=-1, keepdims=True), sink)
            p = jnp.exp(s - m)
            denom = jnp.sum(p, axis=-1, keepdims=True) + jnp.exp(sink - m)
            o = _dot(p.astype(BF16), v_h)
            outs.append(o / denom)
    o_ref[...] = jnp.concatenate(outs, axis=-1).astype(o_ref.dtype)


def _attention(qkv, sinks, bias, bsz, seq):
    t = qkv.shape[0]
    nb = seq // BLOCK
    kcol = A_WIDTH // A_KV_WIDTH
    vcol = kcol + 1

    def cur(b, n):
        return b * nb + n

    def prev(b, n):
        return b * nb + jnp.maximum(n - 1, 0)

    return pl.pallas_call(
        _attn_kernel,
        grid=(bsz, nb),
        in_specs=[pl.BlockSpec(memory_space=pltpu.SMEM),
                  pl.BlockSpec((BLOCK, A_WIDTH), lambda b, n: (cur(b, n), 0)),
                  pl.BlockSpec((BLOCK, A_KV_WIDTH), lambda b, n: (prev(b, n), kcol)),
                  pl.BlockSpec((BLOCK, A_KV_WIDTH), lambda b, n: (cur(b, n), kcol)),
                  pl.BlockSpec((BLOCK, A_KV_WIDTH), lambda b, n: (prev(b, n), vcol)),
                  pl.BlockSpec((BLOCK, A_KV_WIDTH), lambda b, n: (cur(b, n), vcol)),
                  pl.BlockSpec((1, A_Q_HEADS, BLOCK, 2 * BLOCK),
                               lambda b, n: (jnp.minimum(n, 1), 0, 0, 0))],
        out_specs=pl.BlockSpec((BLOCK, A_WIDTH), lambda b, n: (cur(b, n), 0)),
        out_shape=jax.ShapeDtypeStruct((t, A_WIDTH), BF16),
        compiler_params=_cparams("parallel", "parallel"),
        name="swa_attention",
    )(sinks, qkv, qkv, qkv, qkv, qkv, bias)


def _rwkv_prep_kernel(seq, zb_ref, halo_ref, mu_ref, w0_ref, wup_ref, a0_ref, aup_ref,
                      gup_ref, kk_ref, ka_ref, rk_ref,
                      p_ref, bonus_ref, g_ref):
    tm = zb_ref.shape[0]
    i = pl.program_id(0)
    z = zb_ref[...]
    seq_start = (i * tm) % seq == 0
    prev_last = jnp.where(seq_start, 0.0, halo_ref[V7X_SUBLANES - 1:V7X_SUBLANES, :])
    row = lax.broadcasted_iota(jnp.int32, z.shape, 0)
    zsh = jnp.where(row == 0, prev_last, pltpu.roll(z, 1, axis=0))
    zs = z + (zsh - z) * mu_ref[...]
    c = B_WIDTH
    zr, zk, zv = zs[:, :c], zs[:, c:2 * c], zs[:, 2 * c:3 * c]
    zwd = zs[:, 3 * c:3 * c + DECAY_LORA]
    zad = zs[:, 3 * c + DECAY_LORA:3 * c + DECAY_LORA + AAA_LORA]
    zgd = zs[:, 3 * c + DECAY_LORA + AAA_LORA:]

    w = -jax.nn.softplus(-(w0_ref[...] + _dot(jnp.tanh(zwd).astype(BF16), wup_ref[...]))) - 0.5
    decay = jnp.exp(-jnp.exp(w))
    a = jax.nn.sigmoid(a0_ref[...] + _dot(zad.astype(BF16), aup_ref[...]))
    g = _dot(jax.nn.sigmoid(zgd).astype(BF16), gup_ref[...])

    kk = zk * kk_ref[...]
    kk = kk / jnp.maximum(jnp.sqrt(_head_sum(kk * kk)), 1e-12)
    k = zk * (1.0 + (a - 1.0) * ka_ref[...])
    p_ref[0] = zr.T
    p_ref[1] = decay.T
    p_ref[2] = k.T
    p_ref[3] = (-kk).T
    p_ref[4] = (kk * a).T
    p_ref[5] = zv.T
    bonus_ref[...] = _head_sum(zr * k * rk_ref[...]) * zv
    g_ref[...] = g


def _rwkv_prep(zb, seq, mu, w0, w_up, a0, a_up, g_up, k_k, k_a, r_k):
    t, n = zb.shape
    tm = TOKEN_TILE
    hb = tm // V7X_SUBLANES
    c = B_WIDTH

    def row(v):
        return v.reshape(1, -1)

    def full(a):
        return pl.BlockSpec(a.shape, lambda i: (0,) * a.ndim)

    small = [row(mu), row(w0), w_up.astype(BF16), row(a0), a_up.astype(BF16),
             g_up.astype(BF16), row(k_k), row(k_a), row(r_k)]
    tok = pl.BlockSpec((tm, c), lambda i: (i, 0))
    return pl.pallas_call(
        functools.partial(_rwkv_prep_kernel, seq),
        grid=(t // tm,),
        in_specs=[pl.BlockSpec((tm, n), lambda i: (i, 0)),
                  pl.BlockSpec((V7X_SUBLANES, n), lambda i: (jnp.maximum(i * hb - 1, 0), 0))]
                 + [full(a) for a in small],
        out_specs=[pl.BlockSpec((N_SCAN_VECS, c, tm), lambda i: (0, 0, i)), tok, tok],
        out_shape=[jax.ShapeDtypeStruct((N_SCAN_VECS, c, t), F32)]
                  + [jax.ShapeDtypeStruct((t, c), F32)] * 2,
        compiler_params=_cparams("parallel"),
        name="rwkv_prep",
    )(zb, zb, *small)


def _rwkv_scan_kernel(p_hbm, y_hbm, s_ref, pin_ref, yout_ref, slab_ref, sem):
    c = pl.program_id(0)
    nc = pl.num_programs(0)
    n_b = pin_ref.shape[0]
    n_steps = pin_ref.shape[3]
    half = HEAD_DIM // 2
    v_base = HEAD_DIM * N_KEY_VECS
    y_base = v_base + half

    def time_chunk(chunk, b):
        return pl.ds(pl.multiple_of((b * nc + chunk) * n_steps, n_steps), n_steps)

    def in_copy(chunk, b):
        return pltpu.make_async_copy(p_hbm.at[:, :, time_chunk(chunk, b)], pin_ref.at[b], sem.at[0, b])

    def out_copy(chunk, b):
        return pltpu.make_async_copy(yout_ref.at[b], y_hbm.at[:, time_chunk(chunk, b)], sem.at[1, b])

    def slab(index):
        return pl.ds(pl.multiple_of(index * SLAB_ROWS, V7X_SUBLANES), n_steps)

    def feature_rows(dim):
        return pl.ds(pl.multiple_of(dim * B_HEADS, B_HEADS), B_HEADS)

    @pl.when(c == 0)
    def _():
        s_ref[...] = jnp.zeros_like(s_ref)
        for b in range(n_b):
            in_copy(0, b).start()

    for b in range(n_b):
        in_copy(c, b).wait()

    def relayout_keys(j, carry):
        for which in range(N_KEY_VECS):
            x = jnp.concatenate([pin_ref[b, which, feature_rows(j), :] for b in range(n_b)] * 2,
                                axis=0)
            slab_ref[slab(j * N_KEY_VECS + which), :] = x.T
        return carry

    def relayout_values(i, carry):
        x = jnp.concatenate([pin_ref[b, N_KEY_VECS, feature_rows(ih * half + i), :]
                             for ih in range(2) for b in range(n_b)], axis=0)
        slab_ref[slab(v_base + i), :] = x.T
        return carry

    lax.fori_loop(0, HEAD_DIM, relayout_keys, 0, unroll=2)
    lax.fori_loop(0, half, relayout_values, 0, unroll=4)

    @pl.when(c + 1 < nc)
    def _():
        for b in range(n_b):
            in_copy(c + 1, b).start()

    def vec(t, which, j):
        return slab_ref[pl.ds((j * N_KEY_VECS + which) * SLAB_ROWS + t, 1), :]

    def step_rows(base, t):
        return pl.ds(base * SLAB_ROWS + t, half, stride=SLAB_ROWS)

    zero = jnp.zeros((half, V7X_LANES), F32)

    sa_first = [zero, zero]
    for j in range(HEAD_DIM):
        sa_first[j % 2] = sa_first[j % 2] + s_ref[j] * vec(0, 3, j)

    def step(t, sa):
        v_t = slab_ref[step_rows(v_base, t), :]
        t_next = jnp.minimum(t + 1, n_steps - 1)
        y = [zero, zero]
        sa_next = [zero, zero]
        for j in range(HEAD_DIM):
            s = s_ref[j] * vec(t, 1, j) + sa * vec(t, 4, j) + v_t * vec(t, 2, j)
            s_ref[j] = s
            y[j % 2] = y[j % 2] + s * vec(t, 0, j)
            sa_next[j % 2] = sa_next[j % 2] + s * vec(t_next, 3, j)
        slab_ref[step_rows(y_base, t), :] = y[0] + y[1]
        return sa_next[0] + sa_next[1]

    lax.fori_loop(0, n_steps, step, sa_first[0] + sa_first[1])

    @pl.when(c > 0)
    def _():
        for b in range(n_b):
            out_copy(c - 1, b).wait()

    def write_y(i, carry):
        yt = slab_ref[slab(y_base + i), :].T
        for ih in range(2):
            for b in range(n_b):
                r = (ih * n_b + b) * B_HEADS
                yout_ref[b, feature_rows(ih * half + i), :] = yt[r:r + B_HEADS, :]
        return carry

    lax.fori_loop(0, half, write_y, 0, unroll=4)
    for b in range(n_b):
        out_copy(c, b).start()

    @pl.when(c == nc - 1)
    def _():
        for b in range(n_b):
            out_copy(c, b).wait()


def _rwkv_scan(p, bsz):
    n_vec, c, t = p.shape
    seq = t // bsz
    tt = SCAN_CHUNK
    half = HEAD_DIM // 2
    n_slabs = HEAD_DIM * N_KEY_VECS + 2 * half
    return pl.pallas_call(
        _rwkv_scan_kernel,
        grid=(seq // tt,),
        in_specs=[pl.BlockSpec(memory_space=pl.ANY)],
        out_specs=pl.BlockSpec(memory_space=pl.ANY),
        out_shape=jax.ShapeDtypeStruct((c, t), F32),
        scratch_shapes=[pltpu.VMEM((HEAD_DIM, half, V7X_LANES), F32),
                        pltpu.VMEM((bsz, n_vec, c, tt), F32),
                        pltpu.VMEM((bsz, c, tt), F32),
                        pltpu.VMEM((n_slabs * SLAB_ROWS, V7X_LANES), F32),
                        pltpu.SemaphoreType.DMA((2, bsz))],
        compiler_params=_cparams("arbitrary"),
        name="rwkv_scan",
    )(p)


def _mix_out_kernel(ya_ref, y_ref, bonus_ref, g_ref, h_ref, gng_ref, gnb_ref,
                    wa_ref, wb_ref, lng_ref, lnb_ref, o_ref):
    y = y_ref[...].T
    inv = 1.0 / HEAD_DIM
    d = y - _head_sum(y) * inv
    var = _head_sum(d * d) * inv
    yn = d * lax.rsqrt(var + GN_EPS) * gng_ref[...] + gnb_ref[...]
    yb = (yn + bonus_ref[...]) * g_ref[...]
    mix = _dot(ya_ref[...], wa_ref[...]) + _dot(yb.astype(BF16), wb_ref[...])
    o_ref[...] = _layer_norm(ALPHA * h_ref[...] + mix, lng_ref[...], lnb_ref[...])


def _mix_out(ya, y, bonus, g, h, gn_g, gn_b, w_out, ln_g, ln_b):
    t, d = h.shape
    tm = TOKEN_TILE
    c = B_WIDTH
    wa = w_out[:A_WIDTH].astype(BF16)
    wb = w_out[A_WIDTH:].astype(BF16)
    small = [gn_g.reshape(1, -1), gn_b.reshape(1, -1), wa, wb,
             ln_g.reshape(1, -1), ln_b.reshape(1, -1)]
    tok = pl.BlockSpec((tm, c), lambda i: (i, 0))
    return pl.pallas_call(
        _mix_out_kernel,
        grid=(t // tm,),
        in_specs=[tok, pl.BlockSpec((c, tm), lambda i: (0, i)), tok, tok,
                  pl.BlockSpec((tm, d), lambda i: (i, 0))]
                 + [pl.BlockSpec(a.shape, lambda i: (0, 0)) for a in small],
        out_specs=pl.BlockSpec((tm, d), lambda i: (i, 0)),
        out_shape=jax.ShapeDtypeStruct((t, d), F32),
        compiler_params=_cparams("parallel"),
        name="mix_out",
    )(ya, y, bonus, g, h, *small)


def _ffn_kernel(x_ref, wg_ref, wu_ref, wd_ref, lng_ref, lnb_ref, o_ref, xb_ref, acc_ref):
    f = pl.program_id(1)

    @pl.when(f == 0)
    def _():
        xb_ref[...] = x_ref[...].astype(BF16)
        acc_ref[...] = jnp.zeros_like(acc_ref)

    xb = xb_ref[...]
    hid = jax.nn.silu(_dot(xb, wg_ref[...])) * _dot(xb, wu_ref[...])
    acc_ref[...] += _dot(hid.astype(BF16), wd_ref[...])

    @pl.when(f == pl.num_programs(1) - 1)
    def _():
        o_ref[...] = _layer_norm(ALPHA * x_ref[...] + acc_ref[...], lng_ref[...], lnb_ref[...])


def _ffn_chunk(d_ff, limit):
    for unit in (V7X_MXU_DIM, V7X_LANES):
        for n in range(limit // unit, 0, -1):
            if d_ff % (n * unit) == 0:
                return n * unit
    raise ValueError(f"d_ff={d_ff} is not a multiple of {V7X_LANES}")


def _dense_ffn(x, w_gate, w_up, w_down, ln_g, ln_b):
    t, d = x.shape
    d_ff = w_gate.shape[1]
    tm = TOKEN_TILE
    tf = _ffn_chunk(d_ff, DENSE_FF_CHUNK)
    mode = dict(pipeline_mode=pl.Buffered(1)) if tf == d_ff else {}
    return pl.pallas_call(
        _ffn_kernel,
        grid=(t // tm, d_ff // tf),
        in_specs=[pl.BlockSpec((tm, d), lambda i, f: (i, 0)),
                  pl.BlockSpec((d, tf), lambda i, f: (0, f), **mode),
                  pl.BlockSpec((d, tf), lambda i, f: (0, f), **mode),
                  pl.BlockSpec((tf, d), lambda i, f: (f, 0), **mode),
                  pl.BlockSpec((1, d), lambda i, f: (0, 0)),
                  pl.BlockSpec((1, d), lambda i, f: (0, 0))],
        out_specs=pl.BlockSpec((tm, d), lambda i, f: (i, 0)),
        out_shape=jax.ShapeDtypeStruct((t, d), F32),
        scratch_shapes=[pltpu.VMEM((tm, d), BF16), pltpu.VMEM((tm, d), F32)],
        compiler_params=_cparams("parallel", "arbitrary"),
        name="dense_ffn",
    )(x, w_gate.astype(BF16), w_up.astype(BF16), w_down.astype(BF16),
      ln_g.reshape(1, -1), ln_b.reshape(1, -1))


def _conv_mixer_kernel(seq, x_ref, halo_ref, win_ref, cw_ref, wout_ref, lng_ref, lnb_ref, rwt_ref,
                       o_ref, idx_ref, prob_ref):
    tm, d = x_ref.shape
    i = pl.program_id(0)
    x = x_ref[...]
    seq_start = (i * tm) % seq == 0
    xa = jnp.concatenate([halo_ref[...], x], axis=0).astype(BF16)
    z = _dot(xa, win_ref[...])
    gb = z[V7X_SUBLANES:, :d]
    u = z[:, d:2 * d] * z[:, 2 * d:]
    row = lax.broadcasted_iota(jnp.int32, u.shape, 0)
    u = jnp.where((row < V7X_SUBLANES) & seq_start, 0.0, u)
    cw = cw_ref[...]
    conv = pltpu.roll(u, 2, axis=0)[V7X_SUBLANES:] * cw[0:1]
    conv = conv + pltpu.roll(u, 1, axis=0)[V7X_SUBLANES:] * cw[1:2]
    conv = conv + u[V7X_SUBLANES:] * cw[2:3]
    y = _dot((gb * conv).astype(BF16), wout_ref[...])
    h = _layer_norm(ALPHA * x + y, lng_ref[...], lnb_ref[...])
    o_ref[...] = h
    _route_top2(h, rwt_ref, idx_ref, prob_ref)


def _conv_mixer(x, seq, w_in, conv_w, w_out, ln_g, ln_b, router_w):
    t, d = x.shape
    tm = TOKEN_TILE
    hb = tm // V7X_SUBLANES
    small = [w_in.astype(BF16), conv_w, w_out.astype(BF16), ln_g.reshape(1, -1), ln_b.reshape(1, -1),
             router_w.T]
    route = pl.BlockSpec((tm, ROUTER_LANES), lambda i: (i, 0))
    h, idx, prob = pl.pallas_call(
        functools.partial(_conv_mixer_kernel, seq),
        grid=(t // tm,),
        in_specs=[pl.BlockSpec((tm, d), lambda i: (i, 0)),
                  pl.BlockSpec((V7X_SUBLANES, d), lambda i: (jnp.maximum(i * hb - 1, 0), 0))]
                 + [pl.BlockSpec(a.shape, lambda i: (0, 0)) for a in small],
        out_specs=[pl.BlockSpec((tm, d), lambda i: (i, 0)), route, route],
        out_shape=[jax.ShapeDtypeStruct((t, d), F32),
                   jax.ShapeDtypeStruct((t, ROUTER_LANES), jnp.int32),
                   jax.ShapeDtypeStruct((t, ROUTER_LANES), F32)],
        compiler_params=_cparams("parallel"),
        name="conv_mixer",
    )(x, x, *small)
    return h, idx[:, :TOP_K], prob


def _route_top2(x, wt_ref, idx_ref, prob_ref):
    lane = lax.broadcasted_iota(jnp.int32, (x.shape[0], ROUTER_LANES), 1)
    logits = jnp.full((x.shape[0], ROUTER_LANES), -jnp.inf, F32)
    for e in range(N_EXPERTS):
        logit_e = jnp.sum(x * wt_ref[e:e + 1, :], axis=-1, keepdims=True)
        logits = jnp.where(lane == e, logit_e, logits)
    m1 = jnp.max(logits, axis=-1, keepdims=True)
    i1 = jnp.min(jnp.where(logits == m1, lane, ROUTER_LANES), axis=-1, keepdims=True)
    rest = jnp.where(lane == i1, -jnp.inf, logits)
    m2 = jnp.max(rest, axis=-1, keepdims=True)
    i2 = jnp.min(jnp.where(rest == m2, lane, ROUTER_LANES), axis=-1, keepdims=True)
    e2 = jnp.exp(m2 - m1)
    den = 1.0 + e2
    idx_ref[...] = jnp.where(lane == 0, i1, jnp.where(lane == 1, i2, 0))
    prob_ref[...] = jnp.where(lane == 0, 1.0 / den, jnp.where(lane == 1, e2 / den, 0.0))


def _to_slots(slot_ref, x):
    rows = x.shape[0]
    for s in range(V7X_SUBLANES):
        slot_ref[pl.ds(s, rows, stride=SLOT_ROWS), :] = x[:, s * V7X_LANES:(s + 1) * V7X_LANES]
    slot_ref[pl.ds(V7X_SUBLANES, rows, stride=SLOT_ROWS), :] = jnp.zeros((rows, V7X_LANES), x.dtype)


def _from_slots(slot_ref, rows):
    return jnp.concatenate([slot_ref[pl.ds(s, rows, stride=SLOT_ROWS), :] for s in range(V7X_SUBLANES)],
                           axis=-1)


def _dispatch_kernel(cnt_ref, start_ref, end_ref, pos_ref, h_ref, xs_ref, buf_ref, zero_ref, sem):
    tm = h_ref.shape[0]
    rows = xs_ref.shape[0] // SLOT_ROWS
    i = pl.program_id(0)
    cur = i % 2
    zero_sem = 2
    _to_slots(buf_ref.at[cur], h_ref[...])

    def row_copy(r, k):
        return pltpu.make_async_copy(buf_ref.at[cur, pl.ds(r * SLOT_ROWS, SLOT_ROWS)],
                                     xs_ref.at[pl.ds(pos_ref[0, k, r] * SLOT_ROWS, SLOT_ROWS)],
                                     sem.at[cur])

    def wait_tile(which):
        for _ in range(TOP_K):
            pltpu.make_async_copy(buf_ref.at[which], buf_ref.at[which], sem.at[which]).wait()

    def zero_copy(r):
        return pltpu.make_async_copy(zero_ref, xs_ref.at[pl.ds(r * SLOT_ROWS, SLOT_ROWS)],
                                     sem.at[zero_sem])

    def for_each_padding_row(fn):
        for e in range(N_EXPERTS):
            lax.fori_loop(start_ref[e] + cnt_ref[e], end_ref[e], fn, 0)
        lax.fori_loop(end_ref[N_EXPERTS - 1], rows, fn, 0)

    def start(r, c):
        row_copy(r, 0).start(priority=0)
        row_copy(r, 1).start(priority=1)
        return c

    def zero_start(r, c):
        zero_copy(r).start()
        return c

    def zero_wait(r, c):
        zero_copy(r).wait()
        return c

    lax.fori_loop(0, tm, start, 0, unroll=8)

    @pl.when(i == 0)
    def _():
        zero_ref[...] = jnp.zeros_like(zero_ref)
        for_each_padding_row(zero_start)
        for_each_padding_row(zero_wait)

    @pl.when(i > 0)
    def _():
        wait_tile(1 - cur)

    @pl.when(i == pl.num_programs(0) - 1)
    def _():
        wait_tile(cur)


def _moe_dispatch(h, pos_tiles, counts, starts, ends, rows):
    t, d = h.shape
    assert d == V7X_SUBLANES * V7X_LANES, "one token must fill one (8, 128) tile"
    tm = pos_tiles.shape[2]
    return pl.pallas_call(
        _dispatch_kernel,
        grid_spec=pltpu.PrefetchScalarGridSpec(
            num_scalar_prefetch=3,
            grid=(t // tm,),
            in_specs=[pl.BlockSpec((1, TOP_K, tm), lambda i, *_: (i, 0, 0), memory_space=pltpu.SMEM),
                      pl.BlockSpec((tm, d), lambda i, *_: (i, 0))],
            out_specs=pl.BlockSpec(memory_space=pl.ANY),
            scratch_shapes=[pltpu.VMEM((2, tm * SLOT_ROWS, V7X_LANES), F32),
                            pltpu.VMEM((SLOT_ROWS, V7X_LANES), F32),
                            pltpu.SemaphoreType.DMA((3,))]),
        out_shape=jax.ShapeDtypeStruct((rows * SLOT_ROWS, V7X_LANES), F32),
        compiler_params=_cparams("arbitrary"),
        name="moe_dispatch",
    )(counts, starts, ends, pos_tiles, h)


def _moe_kernel(te_ref, nv_ref, x_ref, wg_ref, wu_ref, wd_ref, o_ref, xb_ref, acc_ref):
    i = pl.program_id(0)
    f = pl.program_id(1)
    last = pl.num_programs(1) - 1
    valid = i < nv_ref[0]
    tg = acc_ref.shape[0]

    @pl.when(valid & (f == 0))
    def _():
        xb_ref[...] = _from_slots(x_ref, tg).astype(BF16)
        acc_ref[...] = jnp.zeros_like(acc_ref)

    @pl.when(valid)
    def _():
        x = xb_ref[...]
        hid = jax.nn.silu(_dot(x, wg_ref[0])) * _dot(x, wu_ref[0])
        acc_ref[...] += _dot(hid.astype(BF16), wd_ref[0])

    @pl.when(valid & (f == last))
    def _():
        _to_slots(o_ref, acc_ref[...])

    @pl.when(jnp.logical_not(valid) & (f == last))
    def _():
        o_ref[...] = jnp.zeros_like(o_ref)


def _moe_experts(xs, tile_expert, n_valid, w_gate, w_up, w_down):
    d = V7X_SUBLANES * V7X_LANES
    r = xs.shape[0] // SLOT_ROWS
    d_ff = w_gate.shape[2]
    tg = MOE_ROW_TILE
    tf = _ffn_chunk(d_ff, MOE_FF_CHUNK)
    nf = d_ff // tf

    def f_eff(i, f, nv):
        return jnp.where(i < nv[0], f, nf - 1)

    row_tile = pl.BlockSpec((tg * SLOT_ROWS, V7X_LANES), lambda i, f, te, nv: (i, 0))
    return pl.pallas_call(
        _moe_kernel,
        grid_spec=pltpu.PrefetchScalarGridSpec(
            num_scalar_prefetch=2,
            grid=(r // tg, nf),
            in_specs=[row_tile,
                      pl.BlockSpec((1, d, tf), lambda i, f, te, nv: (te[i], 0, f_eff(i, f, nv))),
                      pl.BlockSpec((1, d, tf), lambda i, f, te, nv: (te[i], 0, f_eff(i, f, nv))),
                      pl.BlockSpec((1, tf, d), lambda i, f, te, nv: (te[i], f_eff(i, f, nv), 0))],
            out_specs=row_tile,
            scratch_shapes=[pltpu.VMEM((tg, d), BF16), pltpu.VMEM((tg, d), F32)]),
        out_shape=jax.ShapeDtypeStruct(xs.shape, F32),
        compiler_params=_cparams("parallel", "arbitrary"),
        name="moe_experts",
    )(tile_expert, n_valid, xs, w_gate, w_up, w_down)


def _combine_kernel(pos_ref, pos_next_ref, h_ref, p_ref, y_ref, lng_ref, lnb_ref, o_ref, buf_ref, sem):
    tm = h_ref.shape[0]
    i = pl.program_id(0)
    cur = i % 2

    def fetch_tile(tile_pos_ref, which):
        def row_copy(r, k):
            return pltpu.make_async_copy(
                y_ref.at[pl.ds(tile_pos_ref[0, k, r] * SLOT_ROWS, SLOT_ROWS)],
                buf_ref.at[which, k, pl.ds(r * SLOT_ROWS, SLOT_ROWS)], sem.at[which])

        def start(r, c):
            row_copy(r, 0).start(priority=0)
            row_copy(r, 1).start(priority=1)
            return c

        lax.fori_loop(0, tm, start, 0, unroll=8)

    @pl.when(i == 0)
    def _():
        fetch_tile(pos_ref, cur)

    @pl.when(i + 1 < pl.num_programs(0))
    def _():
        fetch_tile(pos_next_ref, 1 - cur)

    pltpu.make_async_copy(buf_ref.at[cur], buf_ref.at[cur], sem.at[cur]).wait()
    p = p_ref[...]
    ffn = (p[:, 0:1] * _from_slots(buf_ref.at[cur, 0], tm)
           + p[:, 1:2] * _from_slots(buf_ref.at[cur, 1], tm))
    o_ref[...] = _layer_norm(ALPHA * h_ref[...] + ffn, lng_ref[...], lnb_ref[...])


def _moe_combine(h, top_p, y, pos_tiles, ln_g, ln_b):
    t, d = h.shape
    tm = pos_tiles.shape[2]
    n_tiles = t // tm
    return pl.pallas_call(
        _combine_kernel,
        grid=(n_tiles,),
        in_specs=[pl.BlockSpec((1, TOP_K, tm), lambda i: (i, 0, 0), memory_space=pltpu.SMEM),
                  pl.BlockSpec((1, TOP_K, tm), lambda i: (jnp.minimum(i + 1, n_tiles - 1), 0, 0),
                               memory_space=pltpu.SMEM),
                  pl.BlockSpec((tm, d), lambda i: (i, 0)),
                  pl.BlockSpec((tm, ROUTER_LANES), lambda i: (i, 0)),
                  pl.BlockSpec(memory_space=pl.ANY),
                  pl.BlockSpec((1, d), lambda i: (0, 0)),
                  pl.BlockSpec((1, d), lambda i: (0, 0))],
        out_specs=pl.BlockSpec((tm, d), lambda i: (i, 0)),
        out_shape=jax.ShapeDtypeStruct((t, d), F32),
        scratch_shapes=[pltpu.VMEM((2, TOP_K, tm * SLOT_ROWS, V7X_LANES), F32),
                        pltpu.SemaphoreType.DMA((2,))],
        compiler_params=_cparams("arbitrary"),
        name="moe_combine",
    )(pos_tiles, pos_tiles, h, top_p, y, ln_g.reshape(1, -1), ln_b.reshape(1, -1))


def _route_plan(top_idx, t):
    tg = MOE_ROW_TILE
    e_flat = top_idx.T.reshape(-1)
    onehot = (e_flat[:, None] == jnp.arange(N_EXPERTS)[None, :]).astype(jnp.int32)
    csum = jnp.cumsum(onehot, axis=0)
    counts = csum[-1]
    padded = (counts + tg - 1) // tg * tg
    ends = jnp.cumsum(padded)
    starts = ends - padded
    pos = jnp.sum(onehot * (csum - 1 + starts[None, :]), axis=1)
    rows = TOP_K * t + N_EXPERTS * tg
    tile_start = jnp.arange(rows // tg, dtype=jnp.int32) * tg
    tile_expert = jnp.sum((ends[None, :] <= tile_start[:, None]).astype(jnp.int32), axis=1)
    tile_expert = jnp.minimum(tile_expert, N_EXPERTS - 1)
    n_valid = (ends[N_EXPERTS - 1] // tg).reshape(1)
    tm = GATHER_TILE
    pos_tiles = jnp.transpose(pos.reshape(TOP_K, t // tm, tm), (1, 0, 2))
    return pos_tiles, counts, starts, ends, tile_expert, n_valid, rows


def _moe(h, top_idx, top_p, w_gate, w_up, w_down, ln_g, ln_b):
    pos_tiles, counts, starts, ends, tile_expert, n_valid, rows = _route_plan(top_idx, h.shape[0])
    xs = _moe_dispatch(h, pos_tiles, counts, starts, ends, rows)
    y = _moe_experts(xs, tile_expert, n_valid,
                     w_gate.astype(BF16), w_up.astype(BF16), w_down.astype(BF16))
    return _moe_combine(h, top_p, y, pos_tiles, ln_g, ln_b)


_DIM_MAJOR = (np.arange(B_WIDTH) % B_HEADS) * HEAD_DIM + np.arange(B_WIDTH) // B_HEADS


def _dim_major_params(w_in, mu, w0, w_up, a0, a_up, g_up, k_k, k_a, r_k, gn_g, gn_b, w_out):
    pm = _DIM_MAJOR
    cols = np.concatenate([pm, B_WIDTH + pm, 2 * B_WIDTH + pm, np.arange(3 * B_WIDTH, B_IN_WIDTH)])
    w_in = jnp.concatenate([w_in[:, :QKV_WIDTH], w_in[:, QKV_WIDTH + cols]], axis=1)
    w_out = jnp.concatenate([w_out[:A_WIDTH], w_out[A_WIDTH + pm]], axis=0)
    return (w_in, mu[cols], w0[pm], w_up[:, pm], a0[pm], a_up[:, pm], g_up[:, pm], k_k[pm], k_a[pm],
            r_k.reshape(-1)[pm], gn_g[pm], gn_b[pm], w_out)


def kernel(x, rel_bias_table, even_w_in, even_sinks, rwkv_mu, rwkv_w0, rwkv_w_up, rwkv_a0, rwkv_a_up, rwkv_g_up, rwkv_k_k, rwkv_k_a, rwkv_r_k, rwkv_gn_g, rwkv_gn_b, even_w_out, even_ln_mix_g, even_ln_mix_b, dense_w_gate, dense_w_up, dense_w_down, even_ln_ffn_g, even_ln_ffn_b, odd_w_in, odd_conv_w, odd_w_out, odd_ln_mix_g, odd_ln_mix_b, router_w, moe_w_gate, moe_w_up, moe_w_down, odd_ln_ffn_g, odd_ln_ffn_b):
    bsz, seq, d = x.shape
    assert seq % TOKEN_TILE == 0 and seq % BLOCK == 0 and seq % SCAN_CHUNK == 0
    assert bsz * B_HEADS * 2 == V7X_LANES, "scan layout puts (half, batch, head) on the lanes"
    h = x.reshape(bsz * seq, d)
    bias = _rel_bias(rel_bias_table)
    for layer in range(DEPTH):
        i = layer // 2
        if layer % 2 == 0:
            (w_in, mu, w0, w_up, a0, a_up, g_up, k_k, k_a, r_k, gn_g, gn_b, w_out) = _dim_major_params(
                even_w_in[i], rwkv_mu[i], rwkv_w0[i], rwkv_w_up[i], rwkv_a0[i], rwkv_a_up[i],
                rwkv_g_up[i], rwkv_k_k[i], rwkv_k_a[i], rwkv_r_k[i], rwkv_gn_g[i], rwkv_gn_b[i],
                even_w_out[i])
            qkv, zb = _in_proj(h, w_in.astype(BF16))
            ya = _attention(qkv, even_sinks[i], bias, bsz, seq)
            p, bonus, g = _rwkv_prep(zb, seq, mu, w0, w_up, a0, a_up, g_up, k_k, k_a, r_k)
            y = _rwkv_scan(p, bsz)
            h = _mix_out(ya, y, bonus, g, h, gn_g, gn_b, w_out,
                         even_ln_mix_g[i], even_ln_mix_b[i])
            h = _dense_ffn(h, dense_w_gate[i], dense_w_up[i], dense_w_down[i],
                           even_ln_ffn_g[i], even_ln_ffn_b[i])
        else:
            h, top_idx, top_p = _conv_mixer(h, seq, odd_w_in[i], odd_conv_w[i], odd_w_out[i],
                                            odd_ln_mix_g[i], odd_ln_mix_b[i], router_w[i])
            h = _moe(h, top_idx, top_p, moe_w_gate[i], moe_w_up[i], moe_w_down[i],
                     odd_ln_ffn_g[i], odd_ln_ffn_b[i])
    return h.reshape(bsz, seq, d)
```

```python
import functools
import math

import numpy as np
import jax
import jax.numpy as jnp
from jax import lax
from jax.experimental import pallas as pl
from jax.experimental.pallas import tpu as pltpu

F32 = jnp.float32
BF16 = jnp.bfloat16

HEAD_DIM = 64
A_Q_HEADS = 8
A_KV_HEADS = 2
A_GROUP = A_Q_HEADS // A_KV_HEADS
WINDOW = 128
BLOCK = 128
NUM_BUCKETS = 32
MAX_DISTANCE = 128
B_HEADS = 8
A_WIDTH = A_Q_HEADS * HEAD_DIM
A_KV_WIDTH = A_KV_HEADS * HEAD_DIM
QKV_WIDTH = A_WIDTH + 2 * A_KV_WIDTH
B_WIDTH = B_HEADS * HEAD_DIM
DECAY_LORA = 64
AAA_LORA = 64
GATE_LORA = 128
B_IN_WIDTH = 3 * B_WIDTH + DECAY_LORA + AAA_LORA + GATE_LORA
CONV_WIDTH = 3
N_EXPERTS = 8
TOP_K = 2
DEPTH = 2
ALPHA = (2.0 * DEPTH) ** 0.25
LN_EPS = 1e-5
GN_EPS = 64e-5
NEG_INF = -1e30
ATTN_SCALE = HEAD_DIM ** -0.5

V7X_LANES = 128
V7X_SUBLANES = 8
V7X_MXU_DIM = 256
V7X_VMEM_LIMIT = 56 * 1024 * 1024
MOE_FF_CHUNK = 1792

TOKEN_TILE = 512
SCAN_CHUNK = 128
N_SCAN_VECS = 6
N_KEY_VECS = 5
SLAB_ROWS = SCAN_CHUNK + 8
MOE_ROW_TILE = 512
GATHER_TILE = 512
ROUTER_LANES = 128
SLOT_ROWS = 9


def _cparams(*sem):
    return pltpu.CompilerParams(dimension_semantics=sem, vmem_limit_bytes=V7X_VMEM_LIMIT)


def _layer_norm(y, g, b):
    mu = jnp.mean(y, axis=-1, keepdims=True)
    d = y - mu
    var = jnp.mean(d * d, axis=-1, keepdims=True)
    return d * lax.rsqrt(var + LN_EPS) * g + b


def _dot(a, b):
    return jnp.dot(a, b, preferred_element_type=F32)


def _head_sum(x):
    n = x.shape[1] // V7X_LANES
    s = x[:, :V7X_LANES]
    for c in range(1, n):
        s = s + x[:, c * V7X_LANES:(c + 1) * V7X_LANES]
    shift = B_HEADS
    while shift < V7X_LANES:
        s = s + pltpu.roll(s, shift, axis=1)
        shift *= 2
    return jnp.concatenate([s] * n, axis=1)


def _in_proj_kernel(x_ref, w_ref, qkv_ref, zb_ref):
    z = _dot(x_ref[...].astype(BF16), w_ref[...])
    qkv_ref[...] = z[:, :QKV_WIDTH].astype(BF16)
    zb_ref[...] = z[:, QKV_WIDTH:]


def _in_proj(h, w):
    t, d = h.shape
    n = w.shape[1]
    tm = TOKEN_TILE
    return pl.pallas_call(
        _in_proj_kernel,
        grid=(t // tm,),
        in_specs=[pl.BlockSpec((tm, d), lambda i: (i, 0)),
                  pl.BlockSpec((d, n), lambda i: (0, 0))],
        out_specs=[pl.BlockSpec((tm, QKV_WIDTH), lambda i: (i, 0)),
                   pl.BlockSpec((tm, n - QKV_WIDTH), lambda i: (i, 0))],
        out_shape=[jax.ShapeDtypeStruct((t, QKV_WIDTH), BF16),
                   jax.ShapeDtypeStruct((t, n - QKV_WIDTH), F32)],
        compiler_params=_cparams("parallel"),
        name="in_proj",
    )(h, w)


def _bucket_table():
    qi = np.arange(BLOCK)[:, None]
    ki = np.arange(2 * BLOCK)[None, :]
    n = np.maximum(qi + BLOCK - ki, 0)
    max_exact = NUM_BUCKETS // 2
    log_ratio = (np.log(np.maximum(n, 1).astype(np.float32) / max_exact)
                 / math.log(MAX_DISTANCE / max_exact))
    large = max_exact + (log_ratio * (NUM_BUCKETS - max_exact)).astype(np.int32)
    large = np.minimum(large, NUM_BUCKETS - 1)
    return np.where(n < max_exact, n, large).astype(np.int32)


def _bias_kernel(tab_ref, bucket_ref, out_ref):
    bucket = bucket_ref[...]
    qi = lax.broadcasted_iota(jnp.int32, bucket.shape, 0)
    ki = lax.broadcasted_iota(jnp.int32, bucket.shape, 1)
    dist = qi + BLOCK - ki
    in_window = (dist >= 0) & (dist < WINDOW)
    for h in range(A_Q_HEADS):
        acc = jnp.zeros(bucket.shape, F32)
        for b in range(NUM_BUCKETS):
            acc = jnp.where(bucket == b, tab_ref[b, h], acc)
        out_ref[0, h] = jnp.where(in_window & (ki >= BLOCK), acc, NEG_INF)
        out_ref[1, h] = jnp.where(in_window, acc, NEG_INF)


def _rel_bias(table):
    return pl.pallas_call(
        _bias_kernel,
        in_specs=[pl.BlockSpec(memory_space=pltpu.SMEM),
                  pl.BlockSpec(memory_space=pltpu.VMEM)],
        out_specs=pl.BlockSpec(memory_space=pltpu.VMEM),
        out_shape=jax.ShapeDtypeStruct((2, A_Q_HEADS, BLOCK, 2 * BLOCK), F32),
        name="rel_bias",
    )(table, jnp.asarray(_bucket_table()))


def _attn_kernel(sink_ref, q_ref, kp_ref, kc_ref, vp_ref, vc_ref, bias_ref, o_ref):
    q = q_ref[...] * ATTN_SCALE
    kcat = jnp.concatenate([kp_ref[...], kc_ref[...]], axis=0)
    vcat = jnp.concatenate([vp_ref[...], vc_ref[...]], axis=0)
    outs = []
    for hk in range(A_KV_HEADS):
        k_h = kcat[:, hk * HEAD_DIM:(hk + 1) * HEAD_DIM]
        v_h = vcat[:, hk * HEAD_DIM:(hk + 1) * HEAD_DIM]
        for g in range(A_GROUP):
            hq = hk * A_GROUP + g
            q_h = q[:, hq * HEAD_DIM:(hq + 1) * HEAD_DIM]
            s = lax.dot_general(q_h, k_h, (((1,), (1,)), ((), ())),
                                preferred_element_type=F32) + bias_ref[0, hq]
            sink = sink_ref[hq]
            m = jnp.maximum(jnp.max(s, axis=-1, keepdims=True), sink)
            p = jnp.exp(s - m)
            denom = jnp.sum(p, axis=-1, keepdims=True) + jnp.exp(sink - m)
            o = _dot(p.astype(BF16), v_h)
            outs.append(o / denom)
    o_ref[...] = jnp.concatenate(outs, axis=-1).astype(o_ref.dtype)


def _attention(qkv, sinks, bias, bsz, seq):
    t = qkv.shape[0]
    nb = seq // BLOCK
    kcol = A_WIDTH // A_KV_WIDTH
    vcol = kcol + 1

    def cur(b, n):
        return b * nb + n

    def prev(b, n):
        return b * nb + jnp.maximum(n - 1, 0)

    return pl.pallas_call(
        _attn_kernel,
        grid=(bsz, nb),
        in_specs=[pl.BlockSpec(memory_space=pltpu.SMEM),
                  pl.BlockSpec((BLOCK, A_WIDTH), lambda b, n: (cur(b, n), 0)),
                  pl.BlockSpec((BLOCK, A_KV_WIDTH), lambda b, n: (prev(b, n), kcol)),
                  pl.BlockSpec((BLOCK, A_KV_WIDTH), lambda b, n: (cur(b, n), kcol)),
                  pl.BlockSpec((BLOCK, A_KV_WIDTH), lambda b, n: (prev(b, n), vcol)),
                  pl.BlockSpec((BLOCK, A_KV_WIDTH), lambda b, n: (cur(b, n), vcol)),
                  pl.BlockSpec((1, A_Q_HEADS, BLOCK, 2 * BLOCK),
                               lambda b, n: (jnp.minimum(n, 1), 0, 0, 0))],
        out_specs=pl.BlockSpec((BLOCK, A_WIDTH), lambda b, n: (cur(b, n), 0)),
        out_shape=jax.ShapeDtypeStruct((t, A_WIDTH), BF16),
        compiler_params=_cparams("parallel", "parallel"),
        name="swa_attention",
    )(sinks, qkv, qkv, qkv, qkv, qkv, bias)


def _rwkv_prep_kernel(seq, zb_ref, halo_ref, mu_ref, w0_ref, wup_ref, a0_ref, aup_ref,
                      gup_ref, kk_ref, ka_ref, rk_ref,
                      p_ref, bonus_ref, g_ref):
    tm = zb_ref.shape[0]
    i = pl.program_id(0)
    z = zb_ref[...]
    seq_start = (i * tm) % seq == 0
    prev_last = jnp.where(seq_start, 0.0, halo_ref[V7X_SUBLANES - 1:V7X_SUBLANES, :])
    row = lax.broadcasted_iota(jnp.int32, z.shape, 0)
    zsh = jnp.where(row == 0, prev_last, pltpu.roll(z, 1, axis=0))
    zs = z + (zsh - z) * mu_ref[...]
    c = B_WIDTH
    zr, zk, zv = zs[:, :c], zs[:, c:2 * c], zs[:, 2 * c:3 * c]
    zwd = zs[:, 3 * c:3 * c + DECAY_LORA]
    zad = zs[:, 3 * c + DECAY_LORA:3 * c + DECAY_LORA + AAA_LORA]
    zgd = zs[:, 3 * c + DECAY_LORA + AAA_LORA:]

    w = -jax.nn.softplus(-(w0_ref[...] + _dot(jnp.tanh(zwd).astype(BF16), wup_ref[...]))) - 0.5
    decay = jnp.exp(-jnp.exp(w))
    a = jax.nn.sigmoid(a0_ref[...] + _dot(zad.astype(BF16), aup_ref[...]))
    g = _dot(jax.nn.sigmoid(zgd).astype(BF16), gup_ref[...])

    kk = zk * kk_ref[...]
    kk = kk / jnp.maximum(jnp.sqrt(_head_sum(kk * kk)), 1e-12)
    k = zk * (1.0 + (a - 1.0) * ka_ref[...])
    p_ref[0] = zr.T
    p_ref[1] = decay.T
    p_ref[2] = k.T
    p_ref[3] = (-kk).T
    p_ref[4] = (kk * a).T
    p_ref[5] = zv.T
    bonus_ref[...] = _head_sum(zr * k * rk_ref[...]) * zv
    g_ref[...] = g


def _rwkv_prep(zb, seq, mu, w0, w_up, a0, a_up, g_up, k_k, k_a, r_k):
    t, n = zb.shape
    tm = TOKEN_TILE
    hb = tm // V7X_SUBLANES
    c = B_WIDTH

    def row(v):
        return v.reshape(1, -1)

    def full(a):
        return pl.BlockSpec(a.shape, lambda i: (0,) * a.ndim)

    small = [row(mu), row(w0), w_up.astype(BF16), row(a0), a_up.astype(BF16),
             g_up.astype(BF16), row(k_k), row(k_a), row(r_k)]
    tok = pl.BlockSpec((tm, c), lambda i: (i, 0))
    return pl.pallas_call(
        functools.partial(_rwkv_prep_kernel, seq),
        grid=(t // tm,),
        in_specs=[pl.BlockSpec((tm, n), lambda i: (i, 0)),
                  pl.BlockSpec((V7X_SUBLANES, n), lambda i: (jnp.maximum(i * hb - 1, 0), 0))]
                 + [full(a) for a in small],
        out_specs=[pl.BlockSpec((N_SCAN_VECS, c, tm), lambda i: (0, 0, i)), tok, tok],
        out_shape=[jax.ShapeDtypeStruct((N_SCAN_VECS, c, t), F32)]
                  + [jax.ShapeDtypeStruct((t, c), F32)] * 2,
        compiler_params=_cparams("parallel"),
        name="rwkv_prep",
    )(zb, zb, *small)


def _rwkv_scan_kernel(p_hbm, y_hbm, s_ref, pin_ref, yout_ref, slab_ref, sem):
    c = pl.program_id(0)
    nc = pl.num_programs(0)
    n_b = pin_ref.shape[0]
    n_steps = pin_ref.shape[3]
    half = HEAD_DIM // 2
    v_base = HEAD_DIM * N_KEY_VECS
    y_base = v_base + half

    def time_chunk(chunk, b):
        return pl.ds(pl.multiple_of((b * nc + chunk) * n_steps, n_steps), n_steps)

    def in_copy(chunk, b):
        return pltpu.make_async_copy(p_hbm.at[:, :, time_chunk(chunk, b)], pin_ref.at[b], sem.at[0, b])

    def out_copy(chunk, b):
        return pltpu.make_async_copy(yout_ref.at[b], y_hbm.at[:, time_chunk(chunk, b)], sem.at[1, b])

    def slab(index):
        return pl.ds(pl.multiple_of(index * SLAB_ROWS, V7X_SUBLANES), n_steps)

    def feature_rows(dim):
        return pl.ds(pl.multiple_of(dim * B_HEADS, B_HEADS), B_HEADS)

    @pl.when(c == 0)
    def _():
        s_ref[...] = jnp.zeros_like(s_ref)
        for b in range(n_b):
            in_copy(0, b).start()

    for b in range(n_b):
        in_copy(c, b).wait()

    def relayout_keys(j, carry):
        for which in range(N_KEY_VECS):
            x = jnp.concatenate([pin_ref[b, which, feature_rows(j), :] for b in range(n_b)] * 2,
                                axis=0)
            slab_ref[slab(j * N_KEY_VECS + which), :] = x.T
        return carry

    def relayout_values(i, carry):
        x = jnp.concatenate([pin_ref[b, N_KEY_VECS, feature_rows(ih * half + i), :]
                             for ih in range(2) for b in range(n_b)], axis=0)
        slab_ref[slab(v_base + i), :] = x.T
        return carry

    lax.fori_loop(0, HEAD_DIM, relayout_keys, 0, unroll=2)
    lax.fori_loop(0, half, relayout_values, 0, unroll=4)

    @pl.when(c + 1 < nc)
    def _():
        for b in range(n_b):
            in_copy(c + 1, b).start()

    def vec(t, which, j):
        return slab_ref[pl.ds((j * N_KEY_VECS + which) * SLAB_ROWS + t, 1), :]

    def step_rows(base, t):
        return pl.ds(base * SLAB_ROWS + t, half, stride=SLAB_ROWS)

    zero = jnp.zeros((half, V7X_LANES), F32)

    sa_first = [zero, zero]
    for j in range(HEAD_DIM):
        sa_first[j % 2] = sa_first[j % 2] + s_ref[j] * vec(0, 3, j)

    def step(t, sa):
        v_t = slab_ref[step_rows(v_base, t), :]
        t_next = jnp.minimum(t + 1, n_steps - 1)
        y = [zero, zero]
        sa_next = [zero, zero]
        for j in range(HEAD_DIM):
            s = s_ref[j] * vec(t, 1, j) + sa * vec(t, 4, j) + v_t * vec(t, 2, j)
            s_ref[j] = s
            y[j % 2] = y[j % 2] + s * vec(t, 0, j)
            sa_next[j % 2] = sa_next[j % 2] + s * vec(t_next, 3, j)
        slab_ref[step_rows(y_base, t), :] = y[0] + y[1]
        return sa_next[0] + sa_next[1]

    lax.fori_loop(0, n_steps, step, sa_first[0] + sa_first[1])

    @pl.when(c > 0)
    def _():
        for b in range(n_b):
            out_copy(c - 1, b).wait()

    def write_y(i, carry):
        yt = slab_ref[slab(y_base + i), :].T
        for ih in range(2):
            for b in range(n_b):
                r = (ih * n_b + b) * B_HEADS
                yout_ref[b, feature_rows(ih * half + i), :] = yt[r:r + B_HEADS, :]
        return carry

    lax.fori_loop(0, half, write_y, 0, unroll=4)
    for b in range(n_b):
        out_copy(c, b).start()

    @pl.when(c == nc - 1)
    def _():
        for b in range(n_b):
            out_copy(c, b).wait()


def _rwkv_scan(p, bsz):
    n_vec, c, t = p.shape
    seq = t // bsz
    tt = SCAN_CHUNK
    half = HEAD_DIM // 2
    n_slabs = HEAD_DIM * N_KEY_VECS + 2 * half
    return pl.pallas_call(
        _rwkv_scan_kernel,
        grid=(seq // tt,),
        in_specs=[pl.BlockSpec(memory_space=pl.ANY)],
        out_specs=pl.BlockSpec(memory_space=pl.ANY),
        out_shape=jax.ShapeDtypeStruct((c, t), F32),
        scratch_shapes=[pltpu.VMEM((HEAD_DIM, half, V7X_LANES), F32),
                        pltpu.VMEM((bsz, n_vec, c, tt), F32),
                        pltpu.VMEM((bsz, c, tt), F32),
                        pltpu.VMEM((n_slabs * SLAB_ROWS, V7X_LANES), F32),
                        pltpu.SemaphoreType.DMA((2, bsz))],
        compiler_params=_cparams("arbitrary"),
        name="rwkv_scan",
    )(p)


def _mix_out_kernel(ya_ref, y_ref, bonus_ref, g_ref, h_ref, gng_ref, gnb_ref,
                    wa_ref, wb_ref, lng_ref, lnb_ref, o_ref):
    y = y_ref[...].T
    inv = 1.0 / HEAD_DIM
    d = y - _head_sum(y) * inv
    var = _head_sum(d * d) * inv
    yn = d * lax.rsqrt(var + GN_EPS) * gng_ref[...] + gnb_ref[...]
    yb = (yn + bonus_ref[...]) * g_ref[...]
    mix = _dot(ya_ref[...], wa_ref[...]) + _dot(yb.astype(BF16), wb_ref[...])
    o_ref[...] = _layer_norm(ALPHA * h_ref[...] + mix, lng_ref[...], lnb_ref[...])


def _mix_out(ya, y, bonus, g, h, gn_g, gn_b, w_out, ln_g, ln_b):
    t, d = h.shape
    tm = TOKEN_TILE
    c = B_WIDTH
    wa = w_out[:A_WIDTH].astype(BF16)
    wb = w_out[A_WIDTH:].astype(BF16)
    small = [gn_g.reshape(1, -1), gn_b.reshape(1, -1), wa, wb,
             ln_g.reshape(1, -1), ln_b.reshape(1, -1)]
    tok = pl.BlockSpec((tm, c), lambda i: (i, 0))
    return pl.pallas_call(
        _mix_out_kernel,
        grid=(t // tm,),
        in_specs=[tok, pl.BlockSpec((c, tm), lambda i: (0, i)), tok, tok,
                  pl.BlockSpec((tm, d), lambda i: (i, 0))]
                 + [pl.BlockSpec(a.shape, lambda i: (0, 0)) for a in small],
        out_specs=pl.BlockSpec((tm, d), lambda i: (i, 0)),
        out_shape=jax.ShapeDtypeStruct((t, d), F32),
        compiler_params=_cparams("parallel"),
        name="mix_out",
    )(ya, y, bonus, g, h, *small)


def _ffn_kernel(x_ref, wg_ref, wu_ref, wd_ref, lng_ref, lnb_ref, o_ref):
    x = x_ref[...]
    xb = x.astype(BF16)
    hid = jax.nn.silu(_dot(xb, wg_ref[...])) * _dot(xb, wu_ref[...])
    ffn = _dot(hid.astype(BF16), wd_ref[...])
    o_ref[...] = _layer_norm(ALPHA * x + ffn, lng_ref[...], lnb_ref[...])


def _ffn_chunk(d_ff, limit):
    for unit in (V7X_MXU_DIM, V7X_LANES):
        for n in range(limit // unit, 0, -1):
            if d_ff % (n * unit) == 0:
                return n * unit
    raise ValueError(f"d_ff={d_ff} is not a multiple of {V7X_LANES}")


def _dense_ffn(x, w_gate, w_up, w_down, ln_g, ln_b):
    t, d = x.shape
    d_ff = w_gate.shape[1]
    tm = TOKEN_TILE
    resident = dict(pipeline_mode=pl.Buffered(1))
    return pl.pallas_call(
        _ffn_kernel,
        grid=(t // tm,),
        in_specs=[pl.BlockSpec((tm, d), lambda i: (i, 0)),
                  pl.BlockSpec((d, d_ff), lambda i: (0, 0), **resident),
                  pl.BlockSpec((d, d_ff), lambda i: (0, 0), **resident),
                  pl.BlockSpec((d_ff, d), lambda i: (0, 0), **resident),
                  pl.BlockSpec((1, d), lambda i: (0, 0)),
                  pl.BlockSpec((1, d), lambda i: (0, 0))],
        out_specs=pl.BlockSpec((tm, d), lambda i: (i, 0)),
        out_shape=jax.ShapeDtypeStruct((t, d), F32),
        compiler_params=_cparams("parallel"),
        name="dense_ffn",
    )(x, w_gate.astype(BF16), w_up.astype(BF16), w_down.astype(BF16),
      ln_g.reshape(1, -1), ln_b.reshape(1, -1))


def _conv_mixer_kernel(seq, x_ref, halo_ref, win_ref, cw_ref, wout_ref, lng_ref, lnb_ref, rwt_ref,
                       o_ref, idx_ref, prob_ref):
    tm, d = x_ref.shape
    i = pl.program_id(0)
    x = x_ref[...]
    seq_start = (i * tm) % seq == 0
    xa = jnp.concatenate([halo_ref[...], x], axis=0).astype(BF16)
    z = _dot(xa, win_ref[...])
    gb = z[V7X_SUBLANES:, :d]
    u = z[:, d:2 * d] * z[:, 2 * d:]
    row = lax.broadcasted_iota(jnp.int32, u.shape, 0)
    u = jnp.where((row < V7X_SUBLANES) & seq_start, 0.0, u)
    cw = cw_ref[...]
    conv = pltpu.roll(u, 2, axis=0)[V7X_SUBLANES:] * cw[0:1]
    conv = conv + pltpu.roll(u, 1, axis=0)[V7X_SUBLANES:] * cw[1:2]
    conv = conv + u[V7X_SUBLANES:] * cw[2:3]
    y = _dot((gb * conv).astype(BF16), wout_ref[...])
    h = _layer_norm(ALPHA * x + y, lng_ref[...], lnb_ref[...])
    o_ref[...] = h
    _route_top2(h, rwt_ref, idx_ref, prob_ref)


def _conv_mixer(x, seq, w_in, conv_w, w_out, ln_g, ln_b, router_w):
    t, d = x.shape
    tm = TOKEN_TILE
    hb = tm // V7X_SUBLANES
    small = [w_in.astype(BF16), conv_w, w_out.astype(BF16), ln_g.reshape(1, -1), ln_b.reshape(1, -1),
             router_w.T]
    route = pl.BlockSpec((tm, ROUTER_LANES), lambda i: (i, 0))
    h, idx, prob = pl.pallas_call(
        functools.partial(_conv_mixer_kernel, seq),
        grid=(t // tm,),
        in_specs=[pl.BlockSpec((tm, d), lambda i: (i, 0)),
                  pl.BlockSpec((V7X_SUBLANES, d), lambda i: (jnp.maximum(i * hb - 1, 0), 0))]
                 + [pl.BlockSpec(a.shape, lambda i: (0, 0)) for a in small],
        out_specs=[pl.BlockSpec((tm, d), lambda i: (i, 0)), route, route],
        out_shape=[jax.ShapeDtypeStruct((t, d), F32),
                   jax.ShapeDtypeStruct((t, ROUTER_LANES), jnp.int32),
                   jax.ShapeDtypeStruct((t, ROUTER_LANES), F32)],
        compiler_params=_cparams("parallel"),
        name="conv_mixer",
    )(x, x, *small)
    return h, idx[:, :TOP_K], prob


def _route_top2(x, wt_ref, idx_ref, prob_ref):
    lane = lax.broadcasted_iota(jnp.int32, (x.shape[0], ROUTER_LANES), 1)
    logits = jnp.full((x.shape[0], ROUTER_LANES), -jnp.inf, F32)
    for e in range(N_EXPERTS):
        logit_e = jnp.sum(x * wt_ref[e:e + 1, :], axis=-1, keepdims=True)
        logits = jnp.where(lane == e, logit_e, logits)
    m1 = jnp.max(logits, axis=-1, keepdims=True)
    i1 = jnp.min(jnp.where(logits == m1, lane, ROUTER_LANES), axis=-1, keepdims=True)
    rest = jnp.where(lane == i1, -jnp.inf, logits)
    m2 = jnp.max(rest, axis=-1, keepdims=True)
    i2 = jnp.min(jnp.where(rest == m2, lane, ROUTER_LANES), axis=-1, keepdims=True)
    e2 = jnp.exp(m2 - m1)
    den = 1.0 + e2
    idx_ref[...] = jnp.where(lane == 0, i1, jnp.where(lane == 1, i2, 0))
    prob_ref[...] = jnp.where(lane == 0, 1.0 / den, jnp.where(lane == 1, e2 / den, 0.0))


def _to_slots(slot_ref, x):
    rows = x.shape[0]
    for s in range(V7X_SUBLANES):
        slot_ref[pl.ds(s, rows, stride=SLOT_ROWS), :] = x[:, s * V7X_LANES:(s + 1) * V7X_LANES]
    slot_ref[pl.ds(V7X_SUBLANES, rows, stride=SLOT_ROWS), :] = jnp.zeros((rows, V7X_LANES), x.dtype)


def _from_slots(slot_ref, rows):
    return jnp.concatenate([slot_ref[pl.ds(s, rows, stride=SLOT_ROWS), :] for s in range(V7X_SUBLANES)],
                           axis=-1)


def _dispatch_kernel(cnt_ref, start_ref, end_ref, pos_ref, h_ref, xs_ref, buf_ref, zero_ref, sem):
    tm = h_ref.shape[0]
    rows = xs_ref.shape[0] // SLOT_ROWS
    i = pl.program_id(0)
    cur = i % 2
    zero_sem = 2
    _to_slots(buf_ref.at[cur], h_ref[...])

    def row_copy(r, k):
        return pltpu.make_async_copy(buf_ref.at[cur, pl.ds(r * SLOT_ROWS, SLOT_ROWS)],
                                     xs_ref.at[pl.ds(pos_ref[0, k, r] * SLOT_ROWS, SLOT_ROWS)],
                                     sem.at[cur])

    def wait_tile(which):
        for _ in range(TOP_K):
            pltpu.make_async_copy(buf_ref.at[which], buf_ref.at[which], sem.at[which]).wait()

    def zero_copy(r):
        return pltpu.make_async_copy(zero_ref, xs_ref.at[pl.ds(r * SLOT_ROWS, SLOT_ROWS)],
                                     sem.at[zero_sem])

    def for_each_padding_row(fn):
        for e in range(N_EXPERTS):
            lax.fori_loop(start_ref[e] + cnt_ref[e], end_ref[e], fn, 0)
        lax.fori_loop(end_ref[N_EXPERTS - 1], rows, fn, 0)

    def start(r, c):
        row_copy(r, 0).start(priority=0)
        row_copy(r, 1).start(priority=1)
        return c

    def zero_start(r, c):
        zero_copy(r).start()
        return c

    def zero_wait(r, c):
        zero_copy(r).wait()
        return c

    lax.fori_loop(0, tm, start, 0, unroll=8)

    @pl.when(i == 0)
    def _():
        zero_ref[...] = jnp.zeros_like(zero_ref)
        for_each_padding_row(zero_start)
        for_each_padding_row(zero_wait)

    @pl.when(i > 0)
    def _():
        wait_tile(1 - cur)

    @pl.when(i == pl.num_programs(0) - 1)
    def _():
        wait_tile(cur)


def _moe_dispatch(h, pos_tiles, counts, starts, ends, rows):
    t, d = h.shape
    assert d == V7X_SUBLANES * V7X_LANES, "one token must fill one (8, 128) tile"
    tm = pos_tiles.shape[2]
    return pl.pallas_call(
        _dispatch_kernel,
        grid_spec=pltpu.PrefetchScalarGridSpec(
            num_scalar_prefetch=3,
            grid=(t // tm,),
            in_specs=[pl.BlockSpec((1, TOP_K, tm), lambda i, *_: (i, 0, 0), memory_space=pltpu.SMEM),
                      pl.BlockSpec((tm, d), lambda i, *_: (i, 0))],
            out_specs=pl.BlockSpec(memory_space=pl.ANY),
            scratch_shapes=[pltpu.VMEM((2, tm * SLOT_ROWS, V7X_LANES), F32),
                            pltpu.VMEM((SLOT_ROWS, V7X_LANES), F32),
                            pltpu.SemaphoreType.DMA((3,))]),
        out_shape=jax.ShapeDtypeStruct((rows * SLOT_ROWS, V7X_LANES), F32),
        compiler_params=_cparams("arbitrary"),
        name="moe_dispatch",
    )(counts, starts, ends, pos_tiles, h)


def _moe_kernel(nf, te_ref, nv_ref, x_ref, wg_ref, wu_ref, wd_ref, o_ref, acc_ref):
    i = pl.program_id(0)
    f = pl.program_id(1)
    last = nf - 1
    valid = i < nv_ref[0]
    tg = acc_ref.shape[0]

    def chunk_ffn():
        x = _from_slots(x_ref, tg).astype(BF16)
        hid = jax.nn.silu(_dot(x, wg_ref[0])) * _dot(x, wu_ref[0])
        return _dot(hid.astype(BF16), wd_ref[0])

    if nf == 1:
        @pl.when(valid)
        def _():
            _to_slots(o_ref, chunk_ffn())
    else:
        @pl.when(valid & (f == 0))
        def _():
            acc_ref[...] = chunk_ffn()

        if nf > 2:
            @pl.when(valid & (f > 0) & (f < last))
            def _():
                acc_ref[...] += chunk_ffn()

        @pl.when(valid & (f == last))
        def _():
            _to_slots(o_ref, acc_ref[...] + chunk_ffn())

    @pl.when(jnp.logical_not(valid) & (f == last))
    def _():
        o_ref[...] = jnp.zeros_like(o_ref)


def _moe_experts(xs, tile_expert, n_valid, w_gate, w_up, w_down):
    d = V7X_SUBLANES * V7X_LANES
    r = xs.shape[0] // SLOT_ROWS
    d_ff = w_gate.shape[2]
    tg = MOE_ROW_TILE
    tf = _ffn_chunk(d_ff, MOE_FF_CHUNK)
    nf = d_ff // tf

    def f_eff(i, f, nv):
        return jnp.where(i < nv[0], f, nf - 1)

    row_tile = pl.BlockSpec((tg * SLOT_ROWS, V7X_LANES), lambda i, f, te, nv: (i, 0))
    return pl.pallas_call(
        functools.partial(_moe_kernel, nf),
        grid_spec=pltpu.PrefetchScalarGridSpec(
            num_scalar_prefetch=2,
            grid=(r // tg, nf),
            in_specs=[row_tile,
                      pl.BlockSpec((1, d, tf), lambda i, f, te, nv: (te[i], 0, f_eff(i, f, nv))),
                      pl.BlockSpec((1, d, tf), lambda i, f, te, nv: (te[i], 0, f_eff(i, f, nv))),
                      pl.BlockSpec((1, tf, d), lambda i, f, te, nv: (te[i], f_eff(i, f, nv), 0))],
            out_specs=row_tile,
            scratch_shapes=[pltpu.VMEM((tg, d), F32)]),
        out_shape=jax.ShapeDtypeStruct(xs.shape, F32),
        compiler_params=_cparams("parallel", "arbitrary"),
        name="moe_experts",
    )(tile_expert, n_valid, xs, w_gate, w_up, w_down)


def _combine_kernel(pos_ref, pos_next_ref, h_ref, p_ref, y_ref, lng_ref, lnb_ref, o_ref, buf_ref, sem):
    tm = h_ref.shape[0]
    i = pl.program_id(0)
    cur = i % 2

    def fetch_tile(tile_pos_ref, which):
        def row_copy(r, k):
            return pltpu.make_async_copy(
                y_ref.at[pl.ds(tile_pos_ref[0, k, r] * SLOT_ROWS, SLOT_ROWS)],
                buf_ref.at[which, k, pl.ds(r * SLOT_ROWS, SLOT_ROWS)], sem.at[which])

        def start(r, c):
            row_copy(r, 0).start(priority=0)
            row_copy(r, 1).start(priority=1)
            return c

        lax.fori_loop(0, tm, start, 0, unroll=8)

    @pl.when(i == 0)
    def _():
        fetch_tile(pos_ref, cur)

    @pl.when(i + 1 < pl.num_programs(0))
    def _():
        fetch_tile(pos_next_ref, 1 - cur)

    pltpu.make_async_copy(buf_ref.at[cur], buf_ref.at[cur], sem.at[cur]).wait()
    p = p_ref[...]
    ffn = (p[:, 0:1] * _from_slots(buf_ref.at[cur, 0], tm)
           + p[:, 1:2] * _from_slots(buf_ref.at[cur, 1], tm))
    o_ref[...] = _layer_norm(ALPHA * h_ref[...] + ffn, lng_ref[...], lnb_ref[...])


def _moe_combine(h, top_p, y, pos_tiles, ln_g, ln_b):
    t, d = h.shape
    tm = pos_tiles.shape[2]
    n_tiles = t // tm
    return pl.pallas_call(
        _combine_kernel,
        grid=(n_tiles,),
        in_specs=[pl.BlockSpec((1, TOP_K, tm), lambda i: (i, 0, 0), memory_space=pltpu.SMEM),
                  pl.BlockSpec((1, TOP_K, tm), lambda i: (jnp.minimum(i + 1, n_tiles - 1), 0, 0),
                               memory_space=pltpu.SMEM),
                  pl.BlockSpec((tm, d), lambda i: (i, 0)),
                  pl.BlockSpec((tm, ROUTER_LANES), lambda i: (i, 0)),
                  pl.BlockSpec(memory_space=pl.ANY),
                  pl.BlockSpec((1, d), lambda i: (0, 0)),
                  pl.BlockSpec((1, d), lambda i: (0, 0))],
        out_specs=pl.BlockSpec((tm, d), lambda i: (i, 0)),
        out_shape=jax.ShapeDtypeStruct((t, d), F32),
        scratch_shapes=[pltpu.VMEM((2, TOP_K, tm * SLOT_ROWS, V7X_LANES), F32),
                        pltpu.SemaphoreType.DMA((2,))],
        compiler_params=_cparams("arbitrary"),
        name="moe_combine",
    )(pos_tiles, pos_tiles, h, top_p, y, ln_g.reshape(1, -1), ln_b.reshape(1, -1))


def _route_plan(top_idx, t):
    tg = MOE_ROW_TILE
    e_flat = top_idx.T.reshape(-1)
    onehot = (e_flat[:, None] == jnp.arange(N_EXPERTS)[None, :]).astype(jnp.int32)
    csum = jnp.cumsum(onehot, axis=0)
    counts = csum[-1]
    padded = (counts + tg - 1) // tg * tg
    ends = jnp.cumsum(padded)
    starts = ends - padded
    pos = jnp.sum(onehot * (csum - 1 + starts[None, :]), axis=1)
    rows = TOP_K * t + N_EXPERTS * tg
    tile_start = jnp.arange(rows // tg, dtype=jnp.int32) * tg
    tile_expert = jnp.sum((ends[None, :] <= tile_start[:, None]).astype(jnp.int32), axis=1)
    tile_expert = jnp.minimum(tile_expert, N_EXPERTS - 1)
    n_valid = (ends[N_EXPERTS - 1] // tg).reshape(1)
    tm = GATHER_TILE
    pos_tiles = jnp.transpose(pos.reshape(TOP_K, t // tm, tm), (1, 0, 2))
    return pos_tiles, counts, starts, ends, tile_expert, n_valid, rows


def _moe(h, top_idx, top_p, w_gate, w_up, w_down, ln_g, ln_b):
    pos_tiles, counts, starts, ends, tile_expert, n_valid, rows = _route_plan(top_idx, h.shape[0])
    xs = _moe_dispatch(h, pos_tiles, counts, starts, ends, rows)
    y = _moe_experts(xs, tile_expert, n_valid,
                     w_gate.astype(BF16), w_up.astype(BF16), w_down.astype(BF16))
    return _moe_combine(h, top_p, y, pos_tiles, ln_g, ln_b)


_DIM_MAJOR = (np.arange(B_WIDTH) % B_HEADS) * HEAD_DIM + np.arange(B_WIDTH) // B_HEADS


def _dim_major_params(w_in, mu, w0, w_up, a0, a_up, g_up, k_k, k_a, r_k, gn_g, gn_b, w_out):
    pm = _DIM_MAJOR
    cols = np.concatenate([pm, B_WIDTH + pm, 2 * B_WIDTH + pm, np.arange(3 * B_WIDTH, B_IN_WIDTH)])
    w_in = jnp.concatenate([w_in[:, :QKV_WIDTH], w_in[:, QKV_WIDTH + cols]], axis=1)
    w_out = jnp.concatenate([w_out[:A_WIDTH], w_out[A_WIDTH + pm]], axis=0)
    return (w_in, mu[cols], w0[pm], w_up[:, pm], a0[pm], a_up[:, pm], g_up[:, pm], k_k[pm], k_a[pm],
            r_k.reshape(-1)[pm], gn_g[pm], gn_b[pm], w_out)


def kernel(x, rel_bias_table, even_w_in, even_sinks, rwkv_mu, rwkv_w0, rwkv_w_up, rwkv_a0, rwkv_a_up, rwkv_g_up, rwkv_k_k, rwkv_k_a, rwkv_r_k, rwkv_gn_g, rwkv_gn_b, even_w_out, even_ln_mix_g, even_ln_mix_b, dense_w_gate, dense_w_up, dense_w_down, even_ln_ffn_g, even_ln_ffn_b, odd_w_in, odd_conv_w, odd_w_out, odd_ln_mix_g, odd_ln_mix_b, router_w, moe_w_gate, moe_w_up, moe_w_down, odd_ln_ffn_g, odd_ln_ffn_b):
    bsz, seq, d = x.shape
    assert seq % TOKEN_TILE == 0 and seq % BLOCK == 0 and seq % SCAN_CHUNK == 0
    assert bsz * B_HEADS * 2 == V7X_LANES, "scan layout puts (half, batch, head) on the lanes"
    h = x.reshape(bsz * seq, d)
    bias = _rel_bias(rel_bias_table)
    for layer in range(DEPTH):
        i = layer // 2
        if layer % 2 == 0:
            (w_in, mu, w0, w_up, a0, a_up, g_up, k_k, k_a, r_k, gn_g, gn_b, w_out) = _dim_major_params(
                even_w_in[i], rwkv_mu[i], rwkv_w0[i], rwkv_w_up[i], rwkv_a0[i], rwkv_a_up[i],
                rwkv_g_up[i], rwkv_k_k[i], rwkv_k_a[i], rwkv_r_k[i], rwkv_gn_g[i], rwkv_gn_b[i],
                even_w_out[i])
            qkv, zb = _in_proj(h, w_in.astype(BF16))
            ya = _attention(qkv, even_sinks[i], bias, bsz, seq)
            p, bonus, g = _rwkv_prep(zb, seq, mu, w0, w_up, a0, a_up, g_up, k_k, k_a, r_k)
            y = _rwkv_scan(p, bsz)
            h = _mix_out(ya, y, bonus, g, h, gn_g, gn_b, w_out,
                         even_ln_mix_g[i], even_ln_mix_b[i])
            h = _dense_ffn(h, dense_w_gate[i], dense_w_up[i], dense_w_down[i],
                           even_ln_ffn_g[i], even_ln_ffn_b[i])
        else:
            h, top_idx, top_p = _conv_mixer(h, seq, odd_w_in[i], odd_conv_w[i], odd_w_out[i],
                                            odd_ln_mix_g[i], odd_ln_mix_b[i], router_w[i])
            h = _moe(h, top_idx, top_p, moe_w_gate[i], moe_w_up[i], moe_w_down[i],
                     odd_ln_ffn_g[i], odd_ln_ffn_b[i])
    return h.reshape(bsz, seq, d)
```

```python
import functools
import math

import numpy as np
import jax
import jax.numpy as jnp
from jax import lax
from jax.experimental import pallas as pl
from jax.experimental.pallas import tpu as pltpu

F32 = jnp.float32
BF16 = jnp.bfloat16

HEAD_DIM = 64
A_Q_HEADS = 8
A_KV_HEADS = 2
A_GROUP = A_Q_HEADS // A_KV_HEADS
WINDOW = 128
BLOCK = 128
NUM_BUCKETS = 32
MAX_DISTANCE = 128
B_HEADS = 8
A_WIDTH = A_Q_HEADS * HEAD_DIM
A_KV_WIDTH = A_KV_HEADS * HEAD_DIM
QKV_WIDTH = A_WIDTH + 2 * A_KV_WIDTH
B_WIDTH = B_HEADS * HEAD_DIM
DECAY_LORA = 64
AAA_LORA = 64
GATE_LORA = 128
B_IN_WIDTH = 3 * B_WIDTH + DECAY_LORA + AAA_LORA + GATE_LORA
CONV_WIDTH = 3
N_EXPERTS = 8
TOP_K = 2
DEPTH = 2
ALPHA = (2.0 * DEPTH) ** 0.25
LN_EPS = 1e-5
GN_EPS = 64e-5
NEG_INF = -1e30
ATTN_SCALE = HEAD_DIM ** -0.5

V7X_LANES = 128
V7X_SUBLANES = 8
V7X_MXU_DIM = 256
V7X_VMEM_LIMIT = 56 * 1024 * 1024
MOE_FF_CHUNK = 1792

TOKEN_TILE = 512
SCAN_CHUNK = 128
N_SCAN_VECS = 6
N_KEY_VECS = 5
SLAB_ROWS = SCAN_CHUNK + 8
MOE_ROW_TILE = 512
GATHER_TILE = 512
ROUTER_LANES = 128
SLOT_ROWS = 9


def _cparams(*sem):
    return pltpu.CompilerParams(dimension_semantics=sem, vmem_limit_bytes=V7X_VMEM_LIMIT)


def _layer_norm(y, g, b):
    mu = jnp.mean(y, axis=-1, keepdims=True)
    d = y - mu
    var = jnp.mean(d * d, axis=-1, keepdims=True)
    return d * lax.rsqrt(var + LN_EPS) * g + b


def _dot(a, b):
    return jnp.dot(a, b, preferred_element_type=F32)


def _head_sum(x):
    n = x.shape[1] // V7X_LANES
    s = x[:, :V7X_LANES]
    for c in range(1, n):
        s = s + x[:, c * V7X_LANES:(c + 1) * V7X_LANES]
    shift = B_HEADS
    while shift < V7X_LANES:
        s = s + pltpu.roll(s, shift, axis=1)
        shift *= 2
    return jnp.concatenate([s] * n, axis=1)


def _in_proj_kernel(x_ref, w_ref, qkv_ref, zb_ref):
    z = _dot(x_ref[...].astype(BF16), w_ref[...])
    qkv_ref[...] = z[:, :QKV_WIDTH].astype(BF16)
    zb_ref[...] = z[:, QKV_WIDTH:]


def _in_proj(h, w):
    t, d = h.shape
    n = w.shape[1]
    tm = TOKEN_TILE
    return pl.pallas_call(
        _in_proj_kernel,
        grid=(t // tm,),
        in_specs=[pl.BlockSpec((tm, d), lambda i: (i, 0)),
                  pl.BlockSpec((d, n), lambda i: (0, 0))],
        out_specs=[pl.BlockSpec((tm, QKV_WIDTH), lambda i: (i, 0)),
                   pl.BlockSpec((tm, n - QKV_WIDTH), lambda i: (i, 0))],
        out_shape=[jax.ShapeDtypeStruct((t, QKV_WIDTH), BF16),
                   jax.ShapeDtypeStruct((t, n - QKV_WIDTH), F32)],
        compiler_params=_cparams("parallel"),
        name="in_proj",
    )(h, w)


def _bucket_table():
    qi = np.arange(BLOCK)[:, None]
    ki = np.arange(2 * BLOCK)[None, :]
    n = np.maximum(qi + BLOCK - ki, 0)
    max_exact = NUM_BUCKETS // 2
    log_ratio = (np.log(np.maximum(n, 1).astype(np.float32) / max_exact)
                 / math.log(MAX_DISTANCE / max_exact))
    large = max_exact + (log_ratio * (NUM_BUCKETS - max_exact)).astype(np.int32)
    large = np.minimum(large, NUM_BUCKETS - 1)
    return np.where(n < max_exact, n, large).astype(np.int32)


def _bias_kernel(tab_ref, bucket_ref, out_ref):
    bucket = bucket_ref[...]
    qi = lax.broadcasted_iota(jnp.int32, bucket.shape, 0)
    ki = lax.broadcasted_iota(jnp.int32, bucket.shape, 1)
    dist = qi + BLOCK - ki
    in_window = (dist >= 0) & (dist < WINDOW)
    for h in range(A_Q_HEADS):
        acc = jnp.zeros(bucket.shape, F32)
        for b in range(NUM_BUCKETS):
            acc = jnp.where(bucket == b, tab_ref[b, h], acc)
        out_ref[0, h] = jnp.where(in_window & (ki >= BLOCK), acc, NEG_INF)
        out_ref[1, h] = jnp.where(in_window, acc, NEG_INF)


def _rel_bias(table):
    return pl.pallas_call(
        _bias_kernel,
        in_specs=[pl.BlockSpec(memory_space=pltpu.SMEM),
                  pl.BlockSpec(memory_space=pltpu.VMEM)],
        out_specs=pl.BlockSpec(memory_space=pltpu.VMEM),
        out_shape=jax.ShapeDtypeStruct((2, A_Q_HEADS, BLOCK, 2 * BLOCK), F32),
        name="rel_bias",
    )(table, jnp.asarray(_bucket_table()))


def _attn_kernel(sink_ref, q_ref, kp_ref, kc_ref, vp_ref, vc_ref, bias_ref, o_ref):
    q = q_ref[...] * ATTN_SCALE
    kcat = jnp.concatenate([kp_ref[...], kc_ref[...]], axis=0)
    vcat = jnp.concatenate([vp_ref[...], vc_ref[...]], axis=0)
    outs = []
    for hk in range(A_KV_HEADS):
        k_h = kcat[:, hk * HEAD_DIM:(hk + 1) * HEAD_DIM]
        v_h = vcat[:, hk * HEAD_DIM:(hk + 1) * HEAD_DIM]
        for g in range(A_GROUP):
            hq = hk * A_GROUP + g
            q_h = q[:, hq * HEAD_DIM:(hq + 1) * HEAD_DIM]
            s = lax.dot_general(q_h, k_h, (((1,), (1,)), ((), ())),
                                preferred_element_type=F32) + bias_ref[0, hq]
            sink = sink_ref[hq]
            m = jnp.maximum(jnp.max(s, axis=-1, keepdims=True), sink)
            p = jnp.exp(s - m)
            denom = jnp.sum(p, axis=-1, keepdims=True) + jnp.exp(sink - m)
            o = _dot(p.astype(BF16), v_h)
            outs.append(o / denom)
    o_ref[...] = jnp.concatenate(outs, axis=-1).astype(o_ref.dtype)


def _attention(qkv, sinks, bias, bsz, seq):
    t = qkv.shape[0]
    nb = seq // BLOCK
    kcol = A_WIDTH // A_KV_WIDTH
    vcol = kcol + 1

    def cur(b, n):
        return b * nb + n

    def prev(b, n):
        return b * nb + jnp.maximum(n - 1, 0)

    return pl.pallas_call(
        _attn_kernel,
        grid=(bsz, nb),
        in_specs=[pl.BlockSpec(memory_space=pltpu.SMEM),
                  pl.BlockSpec((BLOCK, A_WIDTH), lambda b, n: (cur(b, n), 0)),
                  pl.BlockSpec((BLOCK, A_KV_WIDTH), lambda b, n: (prev(b, n), kcol)),
                  pl.BlockSpec((BLOCK, A_KV_WIDTH), lambda b, n: (cur(b, n), kcol)),
                  pl.BlockSpec((BLOCK, A_KV_WIDTH), lambda b, n: (prev(b, n), vcol)),
                  pl.BlockSpec((BLOCK, A_KV_WIDTH), lambda b, n: (cur(b, n), vcol)),
                  pl.BlockSpec((1, A_Q_HEADS, BLOCK, 2 * BLOCK),
                               lambda b, n: (jnp.minimum(n, 1), 0, 0, 0))],
        out_specs=pl.BlockSpec((BLOCK, A_WIDTH), lambda b, n: (cur(b, n), 0)),
        out_shape=jax.ShapeDtypeStruct((t, A_WIDTH), BF16),
        compiler_params=_cparams("parallel", "parallel"),
        name="swa_attention",
    )(sinks, qkv, qkv, qkv, qkv, qkv, bias)


def _rwkv_prep_kernel(seq, zb_ref, halo_ref, mu_ref, w0_ref, wup_ref, a0_ref, aup_ref,
                      gup_ref, kk_ref, ka_ref, rk_ref,
                      p_ref, bonus_ref, g_ref):
    tm = zb_ref.shape[0]
    i = pl.program_id(0)
    z = zb_ref[...]
    seq_start = (i * tm) % seq == 0
    prev_last = jnp.where(seq_start, 0.0, halo_ref[V7X_SUBLANES - 1:V7X_SUBLANES, :])
    row = lax.broadcasted_iota(jnp.int32, z.shape, 0)
    zsh = jnp.where(row == 0, prev_last, pltpu.roll(z, 1, axis=0))
    zs = z + (zsh - z) * mu_ref[...]
    c = B_WIDTH
    zr, zk, zv = zs[:, :c], zs[:, c:2 * c], zs[:, 2 * c:3 * c]
    zwd = zs[:, 3 * c:3 * c + DECAY_LORA]
    zad = zs[:, 3 * c + DECAY_LORA:3 * c + DECAY_LORA + AAA_LORA]
    zgd = zs[:, 3 * c + DECAY_LORA + AAA_LORA:]

    w = -jax.nn.softplus(-(w0_ref[...] + _dot(jnp.tanh(zwd).astype(BF16), wup_ref[...]))) - 0.5
    decay = jnp.exp(-jnp.exp(w))
    a = jax.nn.sigmoid(a0_ref[...] + _dot(zad.astype(BF16), aup_ref[...]))
    g = _dot(jax.nn.sigmoid(zgd).astype(BF16), gup_ref[...])

    kk = zk * kk_ref[...]
    kk = kk / jnp.maximum(jnp.sqrt(_head_sum(kk * kk)), 1e-12)
    k = zk * (1.0 + (a - 1.0) * ka_ref[...])
    p_ref[0] = zr.T
    p_ref[1] = decay.T
    p_ref[2] = k.T
    p_ref[3] = (-kk).T
    p_ref[4] = (kk * a).T
    p_ref[5] = zv.T
    bonus_ref[...] = _head_sum(zr * k * rk_ref[...]) * zv
    g_ref[...] = g


def _rwkv_prep(zb, seq, mu, w0, w_up, a0, a_up, g_up, k_k, k_a, r_k):
    t, n = zb.shape
    tm = TOKEN_TILE
    hb = tm // V7X_SUBLANES
    c = B_WIDTH

    def row(v):
        return v.reshape(1, -1)

    def full(a):
        return pl.BlockSpec(a.shape, lambda i: (0,) * a.ndim)

    small = [row(mu), row(w0), w_up.astype(BF16), row(a0), a_up.astype(BF16),
             g_up.astype(BF16), row(k_k), row(k_a), row(r_k)]
    tok = pl.BlockSpec((tm, c), lambda i: (i, 0))
    return pl.pallas_call(
        functools.partial(_rwkv_prep_kernel, seq),
        grid=(t // tm,),
        in_specs=[pl.BlockSpec((tm, n), lambda i: (i, 0)),
                  pl.BlockSpec((V7X_SUBLANES, n), lambda i: (jnp.maximum(i * hb - 1, 0), 0))]
                 + [full(a) for a in small],
        out_specs=[pl.BlockSpec((N_SCAN_VECS, c, tm), lambda i: (0, 0, i)), tok, tok],
        out_shape=[jax.ShapeDtypeStruct((N_SCAN_VECS, c, t), F32)]
                  + [jax.ShapeDtypeStruct((t, c), F32)] * 2,
        compiler_params=_cparams("parallel"),
        name="rwkv_prep",
    )(zb, zb, *small)


def _rwkv_scan_kernel(p_hbm, y_hbm, s_ref, pin_ref, yout_ref, slab_ref, sem):
    c = pl.program_id(0)
    nc = pl.num_programs(0)
    n_b = pin_ref.shape[0]
    n_steps = pin_ref.shape[3]
    half = HEAD_DIM // 2
    v_base = HEAD_DIM * N_KEY_VECS
    y_base = v_base + half

    def time_chunk(chunk, b):
        return pl.ds(pl.multiple_of((b * nc + chunk) * n_steps, n_steps), n_steps)

    def in_copy(chunk, b):
        return pltpu.make_async_copy(p_hbm.at[:, :, time_chunk(chunk, b)], pin_ref.at[b], sem.at[0, b])

    def out_copy(chunk, b):
        return pltpu.make_async_copy(yout_ref.at[b], y_hbm.at[:, time_chunk(chunk, b)], sem.at[1, b])

    def slab(index):
        return pl.ds(pl.multiple_of(index * SLAB_ROWS, V7X_SUBLANES), n_steps)

    def feature_rows(dim):
        return pl.ds(pl.multiple_of(dim * B_HEADS, B_HEADS), B_HEADS)

    @pl.when(c == 0)
    def _():
        s_ref[...] = jnp.zeros_like(s_ref)
        for b in range(n_b):
            in_copy(0, b).start()

    for b in range(n_b):
        in_copy(c, b).wait()

    def relayout_keys(j, carry):
        for which in range(N_KEY_VECS):
            x = jnp.concatenate([pin_ref[b, which, feature_rows(j), :] for b in range(n_b)] * 2,
                                axis=0)
            slab_ref[slab(j * N_KEY_VECS + which), :] = x.T
        return carry

    def relayout_values(i, carry):
        x = jnp.concatenate([pin_ref[b, N_KEY_VECS, feature_rows(ih * half + i), :]
                             for ih in range(2) for b in range(n_b)], axis=0)
        slab_ref[slab(v_base + i), :] = x.T
        return carry

    lax.fori_loop(0, HEAD_DIM, relayout_keys, 0, unroll=4)
    lax.fori_loop(0, half, relayout_values, 0, unroll=8)

    @pl.when(c + 1 < nc)
    def _():
        for b in range(n_b):
            in_copy(c + 1, b).start()

    def vec(t, which, j):
        return slab_ref[pl.ds((j * N_KEY_VECS + which) * SLAB_ROWS + t, 1), :]

    def step_rows(base, t):
        return pl.ds(base * SLAB_ROWS + t, half, stride=SLAB_ROWS)

    zero = jnp.zeros((half, V7X_LANES), F32)

    sa_first = [zero, zero]
    for j in range(HEAD_DIM):
        sa_first[j % 2] = sa_first[j % 2] + s_ref[j] * vec(0, 3, j)

    def step(t, sa):
        v_t = slab_ref[step_rows(v_base, t), :]
        t_next = jnp.minimum(t + 1, n_steps - 1)
        y = [zero, zero]
        sa_next = [zero, zero]
        for j in range(HEAD_DIM):
            s = s_ref[j] * vec(t, 1, j) + sa * vec(t, 4, j) + v_t * vec(t, 2, j)
            s_ref[j] = s
            y[j % 2] = y[j % 2] + s * vec(t, 0, j)
            sa_next[j % 2] = sa_next[j % 2] + s * vec(t_next, 3, j)
        slab_ref[step_rows(y_base, t), :] = y[0] + y[1]
        return sa_next[0] + sa_next[1]

    lax.fori_loop(0, n_steps, step, sa_first[0] + sa_first[1])

    @pl.when(c > 0)
    def _():
        for b in range(n_b):
            out_copy(c - 1, b).wait()

    def write_y(i, carry):
        yt = slab_ref[slab(y_base + i), :].T
        for ih in range(2):
            for b in range(n_b):
                r = (ih * n_b + b) * B_HEADS
                yout_ref[b, feature_rows(ih * half + i), :] = yt[r:r + B_HEADS, :]
        return carry

    lax.fori_loop(0, half, write_y, 0, unroll=8)
    for b in range(n_b):
        out_copy(c, b).start()

    @pl.when(c == nc - 1)
    def _():
        for b in range(n_b):
            out_copy(c, b).wait()


def _rwkv_scan(p, bsz):
    n_vec, c, t = p.shape
    seq = t // bsz
    tt = SCAN_CHUNK
    half = HEAD_DIM // 2
    n_slabs = HEAD_DIM * N_KEY_VECS + 2 * half
    return pl.pallas_call(
        _rwkv_scan_kernel,
        grid=(seq // tt,),
        in_specs=[pl.BlockSpec(memory_space=pl.ANY)],
        out_specs=pl.BlockSpec(memory_space=pl.ANY),
        out_shape=jax.ShapeDtypeStruct((c, t), F32),
        scratch_shapes=[pltpu.VMEM((HEAD_DIM, half, V7X_LANES), F32),
                        pltpu.VMEM((bsz, n_vec, c, tt), F32),
                        pltpu.VMEM((bsz, c, tt), F32),
                        pltpu.VMEM((n_slabs * SLAB_ROWS, V7X_LANES), F32),
                        pltpu.SemaphoreType.DMA((2, bsz))],
        compiler_params=_cparams("arbitrary"),
        name="rwkv_scan",
    )(p)


def _mix_out_kernel(ya_ref, y_ref, bonus_ref, g_ref, h_ref, gng_ref, gnb_ref,
                    wa_ref, wb_ref, lng_ref, lnb_ref, o_ref):
    y = y_ref[...].T
    inv = 1.0 / HEAD_DIM
    d = y - _head_sum(y) * inv
    var = _head_sum(d * d) * inv
    yn = d * lax.rsqrt(var + GN_EPS) * gng_ref[...] + gnb_ref[...]
    yb = (yn + bonus_ref[...]) * g_ref[...]
    mix = _dot(ya_ref[...], wa_ref[...]) + _dot(yb.astype(BF16), wb_ref[...])
    o_ref[...] = _layer_norm(ALPHA * h_ref[...] + mix, lng_ref[...], lnb_ref[...])


def _mix_out(ya, y, bonus, g, h, gn_g, gn_b, w_out, ln_g, ln_b):
    t, d = h.shape
    tm = TOKEN_TILE
    c = B_WIDTH
    wa = w_out[:A_WIDTH].astype(BF16)
    wb = w_out[A_WIDTH:].astype(BF16)
    small = [gn_g.reshape(1, -1), gn_b.reshape(1, -1), wa, wb,
             ln_g.reshape(1, -1), ln_b.reshape(1, -1)]
    tok = pl.BlockSpec((tm, c), lambda i: (i, 0))
    return pl.pallas_call(
        _mix_out_kernel,
        grid=(t // tm,),
        in_specs=[tok, pl.BlockSpec((c, tm), lambda i: (0, i)), tok, tok,
                  pl.BlockSpec((tm, d), lambda i: (i, 0))]
                 + [pl.BlockSpec(a.shape, lambda i: (0, 0)) for a in small],
        out_specs=pl.BlockSpec((tm, d), lambda i: (i, 0)),
        out_shape=jax.ShapeDtypeStruct((t, d), F32),
        compiler_params=_cparams("parallel"),
        name="mix_out",
    )(ya, y, bonus, g, h, *small)


def _ffn_kernel(x_ref, wg_ref, wu_ref, wd_ref, lng_ref, lnb_ref, o_ref):
    x = x_ref[...]
    xb = x.astype(BF16)
    hid = jax.nn.silu(_dot(xb, wg_ref[...])) * _dot(xb, wu_ref[...])
    ffn = _dot(hid.astype(BF16), wd_ref[...])
    o_ref[...] = _layer_norm(ALPHA * x + ffn, lng_ref[...], lnb_ref[...])


def _ffn_chunk(d_ff, limit):
    for unit in (V7X_MXU_DIM, V7X_LANES):
        for n in range(limit // unit, 0, -1):
            if d_ff % (n * unit) == 0:
                return n * unit
    raise ValueError(f"d_ff={d_ff} is not a multiple of {V7X_LANES}")


def _dense_ffn(x, w_gate, w_up, w_down, ln_g, ln_b):
    t, d = x.shape
    d_ff = w_gate.shape[1]
    tm = TOKEN_TILE
    resident = dict(pipeline_mode=pl.Buffered(1))
    return pl.pallas_call(
        _ffn_kernel,
        grid=(t // tm,),
        in_specs=[pl.BlockSpec((tm, d), lambda i: (i, 0)),
                  pl.BlockSpec((d, d_ff), lambda i: (0, 0), **resident),
                  pl.BlockSpec((d, d_ff), lambda i: (0, 0), **resident),
                  pl.BlockSpec((d_ff, d), lambda i: (0, 0), **resident),
                  pl.BlockSpec((1, d), lambda i: (0, 0)),
                  pl.BlockSpec((1, d), lambda i: (0, 0))],
        out_specs=pl.BlockSpec((tm, d), lambda i: (i, 0)),
        out_shape=jax.ShapeDtypeStruct((t, d), F32),
        compiler_params=_cparams("parallel"),
        name="dense_ffn",
    )(x, w_gate.astype(BF16), w_up.astype(BF16), w_down.astype(BF16),
      ln_g.reshape(1, -1), ln_b.reshape(1, -1))


def _conv_mixer_kernel(seq, x_ref, halo_ref, win_ref, cw_ref, wout_ref, lng_ref, lnb_ref, rwt_ref,
                       o_ref, idx_ref, prob_ref):
    tm, d = x_ref.shape
    i = pl.program_id(0)
    x = x_ref[...]
    seq_start = (i * tm) % seq == 0
    xa = jnp.concatenate([halo_ref[...], x], axis=0).astype(BF16)
    z = _dot(xa, win_ref[...])
    gb = z[V7X_SUBLANES:, :d]
    u = z[:, d:2 * d] * z[:, 2 * d:]
    row = lax.broadcasted_iota(jnp.int32, u.shape, 0)
    u = jnp.where((row < V7X_SUBLANES) & seq_start, 0.0, u)
    cw = cw_ref[...]
    conv = pltpu.roll(u, 2, axis=0)[V7X_SUBLANES:] * cw[0:1]
    conv = conv + pltpu.roll(u, 1, axis=0)[V7X_SUBLANES:] * cw[1:2]
    conv = conv + u[V7X_SUBLANES:] * cw[2:3]
    y = _dot((gb * conv).astype(BF16), wout_ref[...])
    h = _layer_norm(ALPHA * x + y, lng_ref[...], lnb_ref[...])
    o_ref[...] = h
    _route_top2(h, rwt_ref, idx_ref, prob_ref)


def _conv_mixer(x, seq, w_in, conv_w, w_out, ln_g, ln_b, router_w):
    t, d = x.shape
    tm = TOKEN_TILE
    hb = tm // V7X_SUBLANES
    small = [w_in.astype(BF16), conv_w, w_out.astype(BF16), ln_g.reshape(1, -1), ln_b.reshape(1, -1),
             router_w.T]
    route = pl.BlockSpec((tm, ROUTER_LANES), lambda i: (i, 0))
    h, idx, prob = pl.pallas_call(
        functools.partial(_conv_mixer_kernel, seq),
        grid=(t // tm,),
        in_specs=[pl.BlockSpec((tm, d), lambda i: (i, 0)),
                  pl.BlockSpec((V7X_SUBLANES, d), lambda i: (jnp.maximum(i * hb - 1, 0), 0))]
                 + [pl.BlockSpec(a.shape, lambda i: (0, 0)) for a in small],
        out_specs=[pl.BlockSpec((tm, d), lambda i: (i, 0)), route, route],
        out_shape=[jax.ShapeDtypeStruct((t, d), F32),
                   jax.ShapeDtypeStruct((t, ROUTER_LANES), jnp.int32),
                   jax.ShapeDtypeStruct((t, ROUTER_LANES), F32)],
        compiler_params=_cparams("parallel"),
        name="conv_mixer",
    )(x, x, *small)
    return h, idx[:, :TOP_K], prob


def _route_top2(x, wt_ref, idx_ref, prob_ref):
    lane = lax.broadcasted_iota(jnp.int32, (x.shape[0], ROUTER_LANES), 1)
    logits = jnp.full((x.shape[0], ROUTER_LANES), -jnp.inf, F32)
    for e in range(N_EXPERTS):
        logit_e = jnp.sum(x * wt_ref[e:e + 1, :], axis=-1, keepdims=True)
        logits = jnp.where(lane == e, logit_e, logits)
    m1 = jnp.max(logits, axis=-1, keepdims=True)
    i1 = jnp.min(jnp.where(logits == m1, lane, ROUTER_LANES), axis=-1, keepdims=True)
    rest = jnp.where(lane == i1, -jnp.inf, logits)
    m2 = jnp.max(rest, axis=-1, keepdims=True)
    i2 = jnp.min(jnp.where(rest == m2, lane, ROUTER_LANES), axis=-1, keepdims=True)
    e2 = jnp.exp(m2 - m1)
    den = 1.0 + e2
    idx_ref[...] = jnp.where(lane == 0, i1, jnp.where(lane == 1, i2, 0))
    prob_ref[...] = jnp.where(lane == 0, 1.0 / den, jnp.where(lane == 1, e2 / den, 0.0))


def _to_slots(slot_ref, x):
    rows = x.shape[0]
    for s in range(V7X_SUBLANES):
        slot_ref[pl.ds(s, rows, stride=SLOT_ROWS), :] = x[:, s * V7X_LANES:(s + 1) * V7X_LANES]
    slot_ref[pl.ds(V7X_SUBLANES, rows, stride=SLOT_ROWS), :] = jnp.zeros((rows, V7X_LANES), x.dtype)


def _from_slots(slot_ref, rows):
    return jnp.concatenate([slot_ref[pl.ds(s, rows, stride=SLOT_ROWS), :] for s in range(V7X_SUBLANES)],
                           axis=-1)


def _dispatch_kernel(cnt_ref, start_ref, end_ref, pos_ref, h_ref, xs_ref, buf_ref, zero_ref, sem):
    tm = h_ref.shape[0]
    rows = xs_ref.shape[0] // SLOT_ROWS
    i = pl.program_id(0)
    cur = i % 2
    zero_sem = 2
    _to_slots(buf_ref.at[cur], h_ref[...])

    def row_copy(r, k):
        return pltpu.make_async_copy(buf_ref.at[cur, pl.ds(r * SLOT_ROWS, SLOT_ROWS)],
                                     xs_ref.at[pl.ds(pos_ref[0, k, r] * SLOT_ROWS, SLOT_ROWS)],
                                     sem.at[cur])

    def wait_tile(which):
        for _ in range(TOP_K):
            pltpu.make_async_copy(buf_ref.at[which], buf_ref.at[which], sem.at[which]).wait()

    def zero_copy(r):
        return pltpu.make_async_copy(zero_ref, xs_ref.at[pl.ds(r * SLOT_ROWS, SLOT_ROWS)],
                                     sem.at[zero_sem])

    def for_each_padding_row(fn):
        for e in range(N_EXPERTS):
            lax.fori_loop(start_ref[e] + cnt_ref[e], end_ref[e], fn, 0)
        lax.fori_loop(end_ref[N_EXPERTS - 1], rows, fn, 0)

    def start(r, c):
        row_copy(r, 0).start(priority=0)
        row_copy(r, 1).start(priority=1)
        return c

    def zero_start(r, c):
        zero_copy(r).start()
        return c

    def zero_wait(r, c):
        zero_copy(r).wait()
        return c

    lax.fori_loop(0, tm, start, 0, unroll=8)

    @pl.when(i == 0)
    def _():
        zero_ref[...] = jnp.zeros_like(zero_ref)
        for_each_padding_row(zero_start)
        for_each_padding_row(zero_wait)

    @pl.when(i > 0)
    def _():
        wait_tile(1 - cur)

    @pl.when(i == pl.num_programs(0) - 1)
    def _():
        wait_tile(cur)


def _moe_dispatch(h, pos_tiles, counts, starts, ends, rows):
    t, d = h.shape
    assert d == V7X_SUBLANES * V7X_LANES, "one token must fill one (8, 128) tile"
    tm = pos_tiles.shape[2]
    return pl.pallas_call(
        _dispatch_kernel,
        grid_spec=pltpu.PrefetchScalarGridSpec(
            num_scalar_prefetch=3,
            grid=(t // tm,),
            in_specs=[pl.BlockSpec((1, TOP_K, tm), lambda i, *_: (i, 0, 0), memory_space=pltpu.SMEM),
                      pl.BlockSpec((tm, d), lambda i, *_: (i, 0))],
            out_specs=pl.BlockSpec(memory_space=pl.ANY),
            scratch_shapes=[pltpu.VMEM((2, tm * SLOT_ROWS, V7X_LANES), F32),
                            pltpu.VMEM((SLOT_ROWS, V7X_LANES), F32),
                            pltpu.SemaphoreType.DMA((3,))]),
        out_shape=jax.ShapeDtypeStruct((rows * SLOT_ROWS, V7X_LANES), F32),
        compiler_params=_cparams("arbitrary"),
        name="moe_dispatch",
    )(counts, starts, ends, pos_tiles, h)


def _moe_kernel(nf, te_ref, nv_ref, x_ref, wg_ref, wu_ref, wd_ref, o_ref, acc_ref):
    i = pl.program_id(0)
    f = pl.program_id(1)
    last = nf - 1
    valid = i < nv_ref[0]
    tg = acc_ref.shape[0]

    def chunk_ffn():
        x = _from_slots(x_ref, tg).astype(BF16)
        hid = jax.nn.silu(_dot(x, wg_ref[0])) * _dot(x, wu_ref[0])
        return _dot(hid.astype(BF16), wd_ref[0])

    if nf == 1:
        @pl.when(valid)
        def _():
            _to_slots(o_ref, chunk_ffn())
    else:
        @pl.when(valid & (f == 0))
        def _():
            acc_ref[...] = chunk_ffn()

        if nf > 2:
            @pl.when(valid & (f > 0) & (f < last))
            def _():
                acc_ref[...] += chunk_ffn()

        @pl.when(valid & (f == last))
        def _():
            _to_slots(o_ref, acc_ref[...] + chunk_ffn())

    @pl.when(jnp.logical_not(valid) & (f == last))
    def _():
        o_ref[...] = jnp.zeros_like(o_ref)


def _moe_experts(xs, tile_expert, n_valid, w_gate, w_up, w_down):
    d = V7X_SUBLANES * V7X_LANES
    r = xs.shape[0] // SLOT_ROWS
    d_ff = w_gate.shape[2]
    tg = MOE_ROW_TILE
    tf = _ffn_chunk(d_ff, MOE_FF_CHUNK)
    nf = d_ff // tf

    def f_eff(i, f, nv):
        return jnp.where(i < nv[0], f, nf - 1)

    row_tile = pl.BlockSpec((tg * SLOT_ROWS, V7X_LANES), lambda i, f, te, nv: (i, 0))
    return pl.pallas_call(
        functools.partial(_moe_kernel, nf),
        grid_spec=pltpu.PrefetchScalarGridSpec(
            num_scalar_prefetch=2,
            grid=(r // tg, nf),
            in_specs=[row_tile,
                      pl.BlockSpec((1, d, tf), lambda i, f, te, nv: (te[i], 0, f_eff(i, f, nv))),
                      pl.BlockSpec((1, d, tf), lambda i, f, te, nv: (te[i], 0, f_eff(i, f, nv))),
                      pl.BlockSpec((1, tf, d), lambda i, f, te, nv: (te[i], f_eff(i, f, nv), 0))],
            out_specs=row_tile,
            scratch_shapes=[pltpu.VMEM((tg, d), F32)]),
        out_shape=jax.ShapeDtypeStruct(xs.shape, F32),
        compiler_params=_cparams("parallel", "arbitrary"),
        name="moe_experts",
    )(tile_expert, n_valid, xs, w_gate, w_up, w_down)


def _combine_kernel(pos_ref, pos_next_ref, h_ref, p_ref, y_ref, lng_ref, lnb_ref, o_ref, buf_ref, sem):
    tm = h_ref.shape[0]
    i = pl.program_id(0)
    cur = i % 2

    def fetch_tile(tile_pos_ref, which):
        def row_copy(r, k):
            return pltpu.make_async_copy(
                y_ref.at[pl.ds(tile_pos_ref[0, k, r] * SLOT_ROWS, SLOT_ROWS)],
                buf_ref.at[which, k, pl.ds(r * SLOT_ROWS, SLOT_ROWS)], sem.at[which])

        def start(r, c):
            row_copy(r, 0).start(priority=0)
            row_copy(r, 1).start(priority=1)
            return c

        lax.fori_loop(0, tm, start, 0, unroll=8)

    @pl.when(i == 0)
    def _():
        fetch_tile(pos_ref, cur)

    @pl.when(i + 1 < pl.num_programs(0))
    def _():
        fetch_tile(pos_next_ref, 1 - cur)

    pltpu.make_async_copy(buf_ref.at[cur], buf_ref.at[cur], sem.at[cur]).wait()
    p = p_ref[...]
    ffn = (p[:, 0:1] * _from_slots(buf_ref.at[cur, 0], tm)
           + p[:, 1:2] * _from_slots(buf_ref.at[cur, 1], tm))
    o_ref[...] = _layer_norm(ALPHA * h_ref[...] + ffn, lng_ref[...], lnb_ref[...])


def _moe_combine(h, top_p, y, pos_tiles, ln_g, ln_b):
    t, d = h.shape
    tm = pos_tiles.shape[2]
    n_tiles = t // tm
    return pl.pallas_call(
        _combine_kernel,
        grid=(n_tiles,),
        in_specs=[pl.BlockSpec((1, TOP_K, tm), lambda i: (i, 0, 0), memory_space=pltpu.SMEM),
                  pl.BlockSpec((1, TOP_K, tm), lambda i: (jnp.minimum(i + 1, n_tiles - 1), 0, 0),
                               memory_space=pltpu.SMEM),
                  pl.BlockSpec((tm, d), lambda i: (i, 0)),
                  pl.BlockSpec((tm, ROUTER_LANES), lambda i: (i, 0)),
                  pl.BlockSpec(memory_space=pl.ANY),
                  pl.BlockSpec((1, d), lambda i: (0, 0)),
                  pl.BlockSpec((1, d), lambda i: (0, 0))],
        out_specs=pl.BlockSpec((tm, d), lambda i: (i, 0)),
        out_shape=jax.ShapeDtypeStruct((t, d), F32),
        scratch_shapes=[pltpu.VMEM((2, TOP_K, tm * SLOT_ROWS, V7X_LANES), F32),
                        pltpu.SemaphoreType.DMA((2,))],
        compiler_params=_cparams("arbitrary"),
        name="moe_combine",
    )(pos_tiles, pos_tiles, h, top_p, y, ln_g.reshape(1, -1), ln_b.reshape(1, -1))


def _route_plan(top_idx, t):
    tg = MOE_ROW_TILE
    e_flat = top_idx.T.reshape(-1)
    onehot = (e_flat[:, None] == jnp.arange(N_EXPERTS)[None, :]).astype(jnp.int32)
    csum = jnp.cumsum(onehot, axis=0)
    counts = csum[-1]
    padded = (counts + tg - 1) // tg * tg
    ends = jnp.cumsum(padded)
    starts = ends - padded
    pos = jnp.sum(onehot * (csum - 1 + starts[None, :]), axis=1)
    rows = TOP_K * t + N_EXPERTS * tg
    tile_start = jnp.arange(rows // tg, dtype=jnp.int32) * tg
    tile_expert = jnp.sum((ends[None, :] <= tile_start[:, None]).astype(jnp.int32), axis=1)
    tile_expert = jnp.minimum(tile_expert, N_EXPERTS - 1)
    n_valid = (ends[N_EXPERTS - 1] // tg).reshape(1)
    tm = GATHER_TILE
    pos_tiles = jnp.transpose(pos.reshape(TOP_K, t // tm, tm), (1, 0, 2))
    return pos_tiles, counts, starts, ends, tile_expert, n_valid, rows


def _moe(h, top_idx, top_p, w_gate, w_up, w_down, ln_g, ln_b):
    pos_tiles, counts, starts, ends, tile_expert, n_valid, rows = _route_plan(top_idx, h.shape[0])
    xs = _moe_dispatch(h, pos_tiles, counts, starts, ends, rows)
    y = _moe_experts(xs, tile_expert, n_valid,
                     w_gate.astype(BF16), w_up.astype(BF16), w_down.astype(BF16))
    return _moe_combine(h, top_p, y, pos_tiles, ln_g, ln_b)


_DIM_MAJOR = (np.arange(B_WIDTH) % B_HEADS) * HEAD_DIM + np.arange(B_WIDTH) // B_HEADS


def _dim_major_params(w_in, mu, w0, w_up, a0, a_up, g_up, k_k, k_a, r_k, gn_g, gn_b, w_out):
    pm = _DIM_MAJOR
    cols = np.concatenate([pm, B_WIDTH + pm, 2 * B_WIDTH + pm, np.arange(3 * B_WIDTH, B_IN_WIDTH)])
    w_in = jnp.concatenate([w_in[:, :QKV_WIDTH], w_in[:, QKV_WIDTH + cols]], axis=1)
    w_out = jnp.concatenate([w_out[:A_WIDTH], w_out[A_WIDTH + pm]], axis=0)
    return (w_in, mu[cols], w0[pm], w_up[:, pm], a0[pm], a_up[:, pm], g_up[:, pm], k_k[pm], k_a[pm],
            r_k.reshape(-1)[pm], gn_g[pm], gn_b[pm], w_out)


def kernel(x, rel_bias_table, even_w_in, even_sinks, rwkv_mu, rwkv_w0, rwkv_w_up, rwkv_a0, rwkv_a_up, rwkv_g_up, rwkv_k_k, rwkv_k_a, rwkv_r_k, rwkv_gn_g, rwkv_gn_b, even_w_out, even_ln_mix_g, even_ln_mix_b, dense_w_gate, dense_w_up, dense_w_down, even_ln_ffn_g, even_ln_ffn_b, odd_w_in, odd_conv_w, odd_w_out, odd_ln_mix_g, odd_ln_mix_b, router_w, moe_w_gate, moe_w_up, moe_w_down, odd_ln_ffn_g, odd_ln_ffn_b):
    bsz, seq, d = x.shape
    assert seq % TOKEN_TILE == 0 and seq % BLOCK == 0 and seq % SCAN_CHUNK == 0
    assert bsz * B_HEADS * 2 == V7X_LANES, "scan layout puts (half, batch, head) on the lanes"
    h = x.reshape(bsz * seq, d)
    bias = _rel_bias(rel_bias_table)
    for layer in range(DEPTH):
        i = layer // 2
        if layer % 2 == 0:
            (w_in, mu, w0, w_up, a0, a_up, g_up, k_k, k_a, r_k, gn_g, gn_b, w_out) = _dim_major_params(
                even_w_in[i], rwkv_mu[i], rwkv_w0[i], rwkv_w_up[i], rwkv_a0[i], rwkv_a_up[i],
                rwkv_g_up[i], rwkv_k_k[i], rwkv_k_a[i], rwkv_r_k[i], rwkv_gn_g[i], rwkv_gn_b[i],
                even_w_out[i])
            qkv, zb = _in_proj(h, w_in.astype(BF16))
            ya = _attention(qkv, even_sinks[i], bias, bsz, seq)
            p, bonus, g = _rwkv_prep(zb, seq, mu, w0, w_up, a0, a_up, g_up, k_k, k_a, r_k)
            y = _rwkv_scan(p, bsz)
            h = _mix_out(ya, y, bonus, g, h, gn_g, gn_b, w_out,
                         even_ln_mix_g[i], even_ln_mix_b[i])
            h = _dense_ffn(h, dense_w_gate[i], dense_w_up[i], dense_w_down[i],
                           even_ln_ffn_g[i], even_ln_ffn_b[i])
        else:
            h, top_idx, top_p = _conv_mixer(h, seq, odd_w_in[i], odd_conv_w[i], odd_w_out[i],
                                            odd_ln_mix_g[i], odd_ln_mix_b[i], router_w[i])
            h = _moe(h, top_idx, top_p, moe_w_gate[i], moe_w_up[i], moe_w_down[i],
                     odd_ln_ffn_g[i], odd_ln_ffn_b[i])
    return h.reshape(bsz, seq, d)
```
